```python
import functools
import jax, jax.numpy as jnp
from jax import lax
import numpy as np

D_MODEL = 2048
BATCH = 16
SEQ = 256
DEPTH = 2
DEC_BATCH = 4
DEC_SEQ = 4096
PAST_LEN = 256

GRID_W = 64
EPS = 1e-6
MIX_W = D_MODEL
CONV_C = 512
CONV_K = 31
RET_H = 4
RET_DK = 128
RET_DV = 128
RET_W = RET_H * RET_DV
RET_CHUNK = 128
RET_EXP_MIN = 5.0
RET_EXP_MAX = 12.0
MLA_H = 8
MLA_DN = 128
MLA_DR = 64
MLA_DV = 128
MLA_Q_LORA = 768
MLA_KV_LORA = 256
MLA_W = MLA_H * MLA_DV
MLA_SCALE = (MLA_DN + MLA_DR) ** -0.5
ROPE_BASE = 10000.0
ROPE_AXIS = MLA_DR // 2
Q_BLOCK = 128
IN_SIZES = (2 * CONV_C, RET_H * RET_DK, RET_H * RET_DK, RET_H * RET_DV, RET_W, MLA_Q_LORA, MLA_KV_LORA, MLA_DR)
IN_COLS = sum(IN_SIZES)
D_FF = 5632
N_EXPERTS = 8
TOP_K = 2
D_EXPERT = 7168
MOE_BLOCK = 128
N_DENSE = (DEPTH + 1) // 2
N_MOE = DEPTH // 2

kernel_name = 'hymba_conv_retnet_mla_diffusion_step'


def rms_norm(x, g):
    xf = x.astype(jnp.float32)
    y = xf * lax.rsqrt(jnp.mean(xf * xf, axis=-1, keepdims=True) + EPS)
    return (y * g.astype(jnp.float32)).astype(x.dtype)


def layer_norm(x, g, b):
    xf = x.astype(jnp.float32)
    mu = jnp.mean(xf, axis=-1, keepdims=True)
    var = jnp.mean(jnp.square(xf - mu), axis=-1, keepdims=True)
    y = (xf - mu) * lax.rsqrt(var + EPS)
    return (y * g.astype(jnp.float32) + b.astype(jnp.float32)).astype(x.dtype)


def head_norm(o, g):
    of = o.astype(jnp.float32)
    mu = jnp.mean(of, axis=-1, keepdims=True)
    var = jnp.mean(jnp.square(of - mu), axis=-1, keepdims=True)
    y = ((of - mu) * lax.rsqrt(var + EPS)).reshape(o.shape[:-2] + (-1,))
    return (y * g.astype(jnp.float32)).astype(o.dtype)


def split_heads(a, n_heads):
    return a.reshape(a.shape[:-1] + (n_heads, a.shape[-1] // n_heads))


def grid_axial_angles(n_tokens):
    rows = n_tokens // GRID_W
    row = jnp.repeat(jnp.arange(rows), GRID_W).astype(jnp.float32)
    col = (jnp.arange(n_tokens) % GRID_W).astype(jnp.float32)
    inv_freq = jnp.power(ROPE_BASE, -jnp.arange(0, ROPE_AXIS, 2, dtype=jnp.float32) / ROPE_AXIS)
    return row[:, None] * inv_freq, col[:, None] * inv_freq


def rotate_pairs(y, ang):
    y1, y2 = jnp.split(y, 2, axis=-1)
    cos, sin = jnp.cos(ang), jnp.sin(ang)
    return jnp.concatenate([y1 * cos - y2 * sin, y2 * cos + y1 * sin], axis=-1)


def axial_rope(x, ang_row, ang_col):
    x_row, x_col = jnp.split(x.astype(jnp.float32), 2, axis=-1)
    return jnp.concatenate([rotate_pairs(x_row, ang_row), rotate_pairs(x_col, ang_col)], axis=-1).astype(x.dtype)


def conformer_conv(z, dw_w, dw_b, ln_g, ln_b):
    a, gate = jnp.split(z, 2, axis=-1)
    u = a * jax.nn.sigmoid(gate)
    u = lax.conv_general_dilated(u, dw_w[:, None, :], window_strides=(1,),
                                 padding=[(CONV_K // 2, CONV_K // 2)],
                                 dimension_numbers=('NWC', 'WIO', 'NWC'),
                                 feature_group_count=CONV_C) + dw_b
    return jax.nn.silu(layer_norm(u, ln_g, ln_b))


def retention_scan(q, k, v, log_gamma, s0):
    B, T, H, _ = q.shape
    dv = v.shape[-1]
    n = T // RET_CHUNK

    def chunks(a):
        return a.astype(jnp.float32).reshape(B, n, RET_CHUNK, H, a.shape[-1]).transpose(1, 0, 3, 2, 4)

    idx = jnp.arange(RET_CHUNK, dtype=jnp.float32)
    diff = idx[:, None] - idx[None, :]
    lg = log_gamma.astype(jnp.float32)
    decay = jnp.where(diff >= 0, jnp.exp(jnp.maximum(diff, 0.0) * lg[:, None, None]), 0.0)
    q_decay = jnp.exp((idx + 1.0) * lg[:, None])[..., None]
    k_decay = jnp.exp((RET_CHUNK - 1.0 - idx) * lg[:, None])[..., None]
    chunk_decay = jnp.exp(RET_CHUNK * lg)[:, None, None]

    def step(s, qkv):
        qc, kc, vc = qkv
        att = jnp.einsum('bhid,bhjd->bhij', qc, kc) * decay
        o = jnp.einsum('bhij,bhjv->bhiv', att, vc) + q_decay * jnp.einsum('bhid,bhdv->bhiv', qc, s)
        s = chunk_decay * s + jnp.einsum('bhjd,bhjv->bhdv', kc * k_decay, vc)
        return s, o

    s, o = lax.scan(step, s0.astype(jnp.float32), (chunks(q), chunks(k), chunks(v)))
    o = o.transpose(1, 0, 3, 2, 4).reshape(B, T, H, dv)
    return o.astype(q.dtype), s


def block_attention(q_nope, q_rope, k_nope, k_rope, v):
    B, Tq, H, _ = q_nope.shape
    nb = Tq // Q_BLOCK

    def blocks(a):
        return jnp.moveaxis(a.reshape((B, nb, Q_BLOCK) + a.shape[2:]), 1, 0)

    def one_block(qs):
        qn, qr = qs
        s = jnp.einsum('bqhd,bkhd->bhqk', qn, k_nope) + jnp.einsum('bqhr,bkr->bhqk', qr, k_rope)
        p = jax.nn.softmax(s.astype(jnp.float32) * MLA_SCALE, axis=-1).astype(v.dtype)
        return jnp.einsum('bhqk,bkhd->bqhd', p, v)

    o = lax.map(one_block, (blocks(q_nope), blocks(q_rope)))
    return jnp.moveaxis(o, 0, 1).reshape(B, Tq, H * v.shape[-1])


def dense_swiglu(h, w_gate, w_up, w_down):
    return (jax.nn.silu(h @ w_gate) * (h @ w_up)) @ w_down


def moe_swiglu(h, w_router, b_router, w_gate, w_up, w_down):
    lead = h.shape[:-1]
    x = h.reshape(-1, h.shape[-1])
    n_tok = x.shape[0]
    n_assign = n_tok * TOP_K
    n_blocks = -(-n_assign // MOE_BLOCK) + N_EXPERTS
    n_slots = n_blocks * MOE_BLOCK
    logits = (x @ w_router).astype(jnp.float32) + b_router.astype(jnp.float32)
    top_logit, top_idx = lax.top_k(logits, TOP_K)
    gates = jax.nn.softmax(top_logit, axis=-1).reshape(-1)
    expert = top_idx.reshape(-1)
    token = jnp.repeat(jnp.arange(n_tok, dtype=jnp.int32), TOP_K)
    order = jnp.argsort(expert)
    e_sorted = expert[order]
    counts = jnp.bincount(expert, length=N_EXPERTS)
    start = jnp.cumsum(counts) - counts
    padded = (counts + MOE_BLOCK - 1) // MOE_BLOCK * MOE_BLOCK
    padded_end = jnp.cumsum(padded)
    padded_start = padded_end - padded
    dest = padded_start[e_sorted] + jnp.arange(n_assign) - start[e_sorted]
    slot_token = jnp.full((n_slots,), n_tok, jnp.int32).at[dest].set(token[order])
    slot_gate = jnp.zeros((n_slots,), jnp.float32).at[dest].set(gates[order])
    block_expert = jnp.minimum(
        jnp.searchsorted(padded_end, jnp.arange(n_blocks) * MOE_BLOCK, side='right'), N_EXPERTS - 1)
    x_pad = jnp.concatenate([x, jnp.zeros((1, x.shape[-1]), x.dtype)], axis=0)
    xs = x_pad[slot_token].reshape(n_blocks, MOE_BLOCK, -1)

    def expert_block(args):
        xb, e = args
        return (jax.nn.silu(xb @ w_gate[e]) * (xb @ w_up[e])) @ w_down[e]

    ys = lax.map(expert_block, (xs, block_expert)).reshape(n_slots, -1)
    out = jax.ops.segment_sum(ys * slot_gate[:, None].astype(ys.dtype), slot_token, num_segments=n_tok + 1)
    return out[:n_tok].reshape(lead + (-1,))


def trunk_layer(x, cond, lw, channel_mixer, ctx):
    B, T, _ = x.shape
    mod = (jax.nn.silu(cond) @ lw['w_mod'] + lw['b_mod'])[..., None, :]
    sh1, sc1, g1, sh2, sc2, g2 = jnp.split(mod, 6, axis=-1)
    h = rms_norm(x, lw['norm'][0]) * (1.0 + sc1) + sh1
    cuts = [int(i) for i in np.cumsum(IN_SIZES)[:-1]]
    conv_in, rq, rk, rv, rg, cq, ckv, kr = jnp.split(h @ lw['w_in'], cuts, axis=-1)

    y_conv = conformer_conv(conv_in, lw['conv_w'], lw['conv_b'], lw['conv_ln_g'], lw['conv_ln_b'])

    q = split_heads(rq, RET_H)
    k = split_heads(rk, RET_H) * (RET_DK ** -0.5)
    v = split_heads(rv, RET_H)
    log_gamma = jax.nn.log_sigmoid(lw['ret_decay'].astype(jnp.float32))
    if ctx is None:
        s0_f = jnp.zeros((B, RET_H, RET_DK, RET_DV), jnp.float32)
        s0_b = s0_f
    else:
        s0_f, s0_b = ctx[2][:, 0], ctx[2][:, 1]
    o_f, s_f = retention_scan(q, k, v, log_gamma[0], s0_f)
    o_b, s_b = retention_scan(q[:, ::-1], k[:, ::-1], v[:, ::-1], log_gamma[1], s0_b)
    y_ret = head_norm(o_f + o_b[:, ::-1], lw['ret_gn']) * jax.nn.silu(rg)

    qa = split_heads(rms_norm(cq, lw['q_norm']) @ lw['w_uq'], MLA_H)
    q_nope, q_rope = qa[..., :MLA_DN], qa[..., MLA_DN:]
    ckv_n = rms_norm(ckv, lw['kv_norm'])
    kv = split_heads(ckv_n @ lw['w_ukv'], MLA_H)
    k_nope, v_mla = kv[..., :MLA_DN], kv[..., MLA_DN:]
    if ctx is None:
        y_mla = block_attention(q_nope, q_rope, k_nope, kr, v_mla)
        new_state = (ckv_n, kr, jnp.stack([s_f, s_b], axis=1))
    else:
        ckv_ctx, kpe_ctx, _ = ctx
        ang_row, ang_col = grid_axial_angles(T)
        q_rope = axial_rope(q_rope, ang_row[:, None, :], ang_col[:, None, :])
        kr_lat = axial_rope(kr, ang_row, ang_col)
        kv_ctx = split_heads(ckv_ctx @ lw['w_ukv'], MLA_H)
        k_all = jnp.concatenate([kv_ctx[..., :MLA_DN], k_nope], axis=1)
        kr_all = jnp.concatenate([kpe_ctx, kr_lat], axis=1)
        v_all = jnp.concatenate([kv_ctx[..., MLA_DN:], v_mla], axis=1)
        y_mla = block_attention(q_nope, q_rope, k_all, kr_all, v_all)
        new_state = None

    y = jnp.concatenate([y_conv, y_ret, y_mla], axis=-1) @ lw['w_out']
    x = x + g1 * rms_norm(y, lw['norm'][1])
    h = rms_norm(x, lw['norm'][2]) * (1.0 + sc2) + sh2
    x = x + g2 * rms_norm(channel_mixer(h), lw['norm'][3])
    return x, new_state


def setup_inputs(seed: int = 0) -> dict:
    key = jax.random.key(seed)
    k = jax.random.split(key, 30)
    f32 = jnp.float32
    D = D_MODEL

    def nrm(i, shape, scale=1.0):
        return jax.random.normal(k[i], shape, f32) * scale

    def gain(i, shape):
        return 1.0 + nrm(i, shape, 0.05)

    decay_base = jnp.log(jnp.exp2(jnp.linspace(RET_EXP_MIN, RET_EXP_MAX, RET_H)) - 1.0)
    return {
        'x_prompt': nrm(0, (BATCH, SEQ, D)),
        'x_sample': nrm(1, (DEC_BATCH, DEC_SEQ, D)),
        'cache_mla_ckv': nrm(2, (DEC_BATCH, DEPTH, PAST_LEN, MLA_KV_LORA)),
        'cache_mla_kpe': nrm(3, (DEC_BATCH, DEPTH, PAST_LEN, MLA_DR)),
        'state_ret': nrm(4, (DEC_BATCH, DEPTH, 2, RET_H, RET_DK, RET_DV)),
        'c': nrm(5, (DEC_BATCH, D)),
        'c_ctx': nrm(6, (D,)),
        'w_mod': nrm(7, (DEPTH, D, 6 * D), 0.5 * D ** -0.5),
        'b_mod': nrm(8, (DEPTH, 6 * D), 0.02),
        'norm_gains': gain(9, (DEPTH, 4, D)),
        'w_in': nrm(10, (DEPTH, D, IN_COLS), D ** -0.5),
        'w_out': nrm(11, (DEPTH, MIX_W, D), MIX_W ** -0.5),
        'conv_w': nrm(12, (DEPTH, CONV_K, CONV_C), CONV_K ** -0.5),
        'conv_b': nrm(13, (DEPTH, CONV_C), 0.02),
        'conv_ln_g': gain(14, (DEPTH, CONV_C)),
        'conv_ln_b': nrm(15, (DEPTH, CONV_C), 0.02),
        'ret_decay_logit': decay_base + nrm(16, (DEPTH, 2, RET_H), 0.1),
        'ret_gn_g': gain(17, (DEPTH, RET_W)),
        'mla_q_norm': gain(18, (DEPTH, MLA_Q_LORA)),
        'mla_w_uq': nrm(19, (DEPTH, MLA_Q_LORA, MLA_H * (MLA_DN + MLA_DR)), MLA_Q_LORA ** -0.5),
        'mla_kv_norm': gain(20, (DEPTH, MLA_KV_LORA)),
        'mla_w_ukv': nrm(21, (DEPTH, MLA_KV_LORA, MLA_H * (MLA_DN + MLA_DV)), MLA_KV_LORA ** -0.5),
        'ffn_w_gate': nrm(22, (N_DENSE, D, D_FF), D ** -0.5),
        'ffn_w_up': nrm(23, (N_DENSE, D, D_FF), D ** -0.5),
        'ffn_w_down': nrm(24, (N_DENSE, D_FF, D), D_FF ** -0.5),
        'moe_w_router': nrm(25, (N_MOE, D, N_EXPERTS), D ** -0.5),
        'moe_b_router': nrm(26, (N_MOE, N_EXPERTS), 0.01),
        'moe_w_gate': nrm(27, (N_MOE, N_EXPERTS, D, D_EXPERT), D ** -0.5),
        'moe_w_up': nrm(28, (N_MOE, N_EXPERTS, D, D_EXPERT), D ** -0.5),
        'moe_w_down': nrm(29, (N_MOE, N_EXPERTS, D_EXPERT, D), D_EXPERT ** -0.5),
    }


def reference(x_prompt, x_sample, cache_mla_ckv, cache_mla_kpe, state_ret, c, c_ctx,
              w_mod, b_mod, norm_gains, w_in, w_out, conv_w, conv_b, conv_ln_g, conv_ln_b,
              ret_decay_logit, ret_gn_g, mla_q_norm, mla_w_uq, mla_kv_norm, mla_w_ukv,
              ffn_w_gate, ffn_w_up, ffn_w_down,
              moe_w_router, moe_b_router, moe_w_gate, moe_w_up, moe_w_down):
    def layer_weights(l):
        return {'w_mod': w_mod[l], 'b_mod': b_mod[l], 'norm': norm_gains[l],
                'w_in': w_in[l], 'w_out': w_out[l],
                'conv_w': conv_w[l], 'conv_b': conv_b[l], 'conv_ln_g': conv_ln_g[l], 'conv_ln_b': conv_ln_b[l],
                'ret_decay': ret_decay_logit[l], 'ret_gn': ret_gn_g[l],
                'q_norm': mla_q_norm[l], 'w_uq': mla_w_uq[l], 'kv_norm': mla_kv_norm[l], 'w_ukv': mla_w_ukv[l]}

    def channel_mixer(l):
        i = l // 2
        if l % 2 == 0:
            return functools.partial(dense_swiglu, w_gate=ffn_w_gate[i], w_up=ffn_w_up[i], w_down=ffn_w_down[i])
        return functools.partial(moe_swiglu, w_router=moe_w_router[i], b_router=moe_b_router[i],
                                 w_gate=moe_w_gate[i], w_up=moe_w_up[i], w_down=moe_w_down[i])

    y_prompt = x_prompt
    ckv_layers, kpe_layers, ret_layers = [], [], []
    for l in range(DEPTH):
        y_prompt, (ckv_l, kpe_l, ret_l) = trunk_layer(y_prompt, c_ctx, layer_weights(l), channel_mixer(l), None)
        ckv_layers.append(ckv_l)
        kpe_layers.append(kpe_l)
        ret_layers.append(ret_l)
    new_mla_ckv = jnp.stack(ckv_layers, axis=1)
    new_mla_kpe = jnp.stack(kpe_layers, axis=1)
    new_state_ret = jnp.stack(ret_layers, axis=1)

    y_sample = x_sample
    for l in range(DEPTH):
        ctx = (cache_mla_ckv[:, l], cache_mla_kpe[:, l], state_ret[:, l])
        y_sample, _ = trunk_layer(y_sample, c, layer_weights(l), channel_mixer(l), ctx)

    return (y_prompt, y_sample, new_mla_ckv, new_mla_kpe, new_state_ret)
```

```python
import functools

import jax
import jax.numpy as jnp
from jax import lax
from jax.experimental import pallas as pl
from jax.experimental.pallas import tpu as pltpu

F32 = jnp.float32
BF16 = jnp.bfloat16

D_MODEL = 2048
BATCH = 16
SEQ = 256
DEPTH = 2
DEC_BATCH = 4
DEC_SEQ = 4096
PAST_LEN = 256
GRID_W = 64
EPS = 1e-6
CONV_C = 512
CONV_K = 31
RET_H = 4
RET_DK = 128
RET_DV = 128
MLA_H = 8
MLA_DN = 128
MLA_DR = 64
MLA_DV = 128
MLA_Q_LORA = 768
MLA_KV_LORA = 256
MLA_SCALE = (MLA_DN + MLA_DR) ** -0.5
ROPE_BASE = 10000.0
ROPE_AXIS = MLA_DR // 2
D_FF = 5632
N_EXPERTS = 8
TOP_K = 2
D_EXPERT = 7168

LANES = 128
HALO = 16
QK_PAD = 256

ZC_CONV = 0
ZC_RQ = 2 * CONV_C
ZC_RK = ZC_RQ + RET_H * RET_DK
ZC_RV = ZC_RK + RET_H * RET_DK
ZC_RG = ZC_RV + RET_H * RET_DV
ZC_CQ = ZC_RG + RET_H * RET_DV
ZC_CKV = ZC_CQ + MLA_Q_LORA
ZC_KR = ZC_CKV + MLA_KV_LORA
IN_COLS = ZC_KR + MLA_DR
ZW = 4608


def _params(*sem):
    return pltpu.CompilerParams(dimension_semantics=sem)


def _silu(x):
    return x * jax.nn.sigmoid(x)


def _rms(x, g):
    return x * lax.rsqrt(jnp.mean(x * x, axis=-1, keepdims=True) + EPS) * g


def _mod_kernel(c_ref, w_ref, b_ref, o_ref):
    c = c_ref[...]
    a = _silu(c).astype(BF16)
    o_ref[...] = jnp.dot(a, w_ref[...].astype(BF16), preferred_element_type=F32) + b_ref[...]


def _modulation(cond8, w_mod, b_mod, tn=1024):
    L, D, N = w_mod.shape
    return pl.pallas_call(
        _mod_kernel,
        out_shape=jax.ShapeDtypeStruct((L, 8, N), F32),
        grid=(L, N // tn),
        in_specs=[
            pl.BlockSpec((8, D), lambda l, j: (0, 0)),
            pl.BlockSpec((None, D, tn), lambda l, j: (l, 0, j)),
            pl.BlockSpec((None, 1, tn), lambda l, j: (l, 0, j)),
        ],
        out_specs=pl.BlockSpec((None, 8, tn), lambda l, j: (l, 0, j)),
        compiler_params=_params("parallel", "arbitrary"),
        name="modulation",
    )(cond8, w_mod, b_mod.reshape(L, 1, N))


def _inproj_kernel(x_ref, g_ref, sc_ref, sh_ref, w_ref, o_ref, h_scr):
    @pl.when(pl.program_id(1) == 0)
    def _():
        h = _rms(x_ref[...], g_ref[...]) * (1.0 + sc_ref[...]) + sh_ref[...]
        h_scr[...] = h.astype(BF16)

    o_ref[...] = jnp.dot(h_scr[...], w_ref[...], preferred_element_type=F32)


def _mod_spec(chunk, tm, seg):
    return pl.BlockSpec((None, None, 1, D_MODEL), lambda i, *_: (i // (seg // tm), chunk, 0, 0))


def _gain_spec(k):
    return pl.BlockSpec((None, 1, D_MODEL), lambda i, *_: (k, 0, 0))


def _inproj(x, gains, mod, w_in_p, seg, tm=1024, tn=512):
    tok, D = x.shape
    zw = w_in_p.shape[1]
    return pl.pallas_call(
        _inproj_kernel,
        out_shape=jax.ShapeDtypeStruct((tok, zw), F32),
        grid=(tok // tm, zw // tn),
        in_specs=[
            pl.BlockSpec((tm, D), lambda i, j: (i, 0)),
            _gain_spec(0),
            _mod_spec(1, tm, seg),
            _mod_spec(0, tm, seg),
            pl.BlockSpec((D, tn), lambda i, j: (0, j)),
        ],
        out_specs=pl.BlockSpec((tm, tn), lambda i, j: (i, j)),
        scratch_shapes=[pltpu.VMEM((tm, D), BF16)],
        compiler_params=_params("parallel", "arbitrary"),
        name="inproj",
    )(x, gains, mod, mod, w_in_p)


def _seq_pos(t, n_a, per_a, per_b):
    in_a = t < n_a
    tb = t - n_a
    seq = jnp.where(in_a, t // per_a, n_a // per_a + tb // per_b)
    pos = jnp.where(in_a, t % per_a, tb % per_b)
    n = jnp.where(in_a, per_a, per_b)
    return seq, pos, n


def _conv_kernel(zc_ref, zp_ref, zn_ref, w_ref, b_ref, g_ref, bb_ref, o_ref, ubuf, cbuf,
                 *, tt, n_a, per_a, per_b, rows):
    _, pos, n = _seq_pos(pl.program_id(0), n_a, per_a, per_b)
    C = CONV_C

    def glu(z):
        return z[:, :C] * jax.nn.sigmoid(z[:, C:])

    ubuf[0:HALO, :] = jnp.where(pos == 0, 0.0, glu(zp_ref[...]))
    ubuf[HALO:HALO + tt, :] = glu(zc_ref[...])
    ubuf[HALO + tt:2 * HALO + tt, :] = jnp.where(pos == n - 1, 0.0, glu(zn_ref[...]))

    base = HALO - CONV_K // 2
    for cb in range(C // LANES):
        cols = slice(cb * LANES, (cb + 1) * LANES)
        w = w_ref[:, cols]
        bias = b_ref[:, cols]
        for rb in range(tt // rows):
            r0 = rb * rows
            acc = jnp.zeros((rows, LANES), F32)
            for k in range(CONV_K):
                acc = acc + ubuf[r0 + base + k:r0 + base + k + rows, cols] * w[k:k + 1, :]
            cbuf[r0:r0 + rows, cols] = acc + bias

    u = cbuf[...]
    mu = jnp.mean(u, axis=-1, keepdims=True)
    d = u - mu
    var = jnp.mean(d * d, axis=-1, keepdims=True)
    y = d * lax.rsqrt(var + EPS) * g_ref[...] + bb_ref[...]
    o_ref[...] = _silu(y).astype(o_ref.dtype)


def _conv_module(z, conv_w, conv_b, ln_g, ln_b, n_a_tok, len_a, len_b, tt=256, rows=64):
    tok = z.shape[0]
    C = CONV_C
    hb = tt // HALO
    nhb = tok // HALO
    w_p = jnp.concatenate([conv_w, jnp.zeros((32 - CONV_K, C), F32)], axis=0)
    kern = functools.partial(_conv_kernel, tt=tt, n_a=n_a_tok // tt, per_a=len_a // tt,
                             per_b=len_b // tt, rows=rows)
    vec = pl.BlockSpec((1, C), lambda i: (0, 0))
    return pl.pallas_call(
        kern,
        out_shape=jax.ShapeDtypeStruct((tok, C), BF16),
        grid=(tok // tt,),
        in_specs=[
            pl.BlockSpec((tt, 2 * C), lambda i: (i, 0)),
            pl.BlockSpec((HALO, 2 * C), lambda i: (jnp.maximum(i * hb - 1, 0), 0)),
            pl.BlockSpec((HALO, 2 * C), lambda i: (jnp.minimum((i + 1) * hb, nhb - 1), 0)),
            pl.BlockSpec((32, C), lambda i: (0, 0)),
            vec, vec, vec,
        ],
        out_specs=pl.BlockSpec((tt, C), lambda i: (i, 0)),
        scratch_shapes=[pltpu.VMEM((tt + 2 * HALO, C), F32), pltpu.VMEM((tt, C), F32)],
        compiler_params=_params("parallel"),
        name="conv_module",
    )(z, z, z, w_p, conv_b.reshape(1, C), ln_g.reshape(1, C), ln_b.reshape(1, C))


def _log_sigmoid(x):
    return -(jnp.maximum(-x, 0.0) + jnp.log(1.0 + jnp.exp(-jnp.abs(x))))


def _ret_kernel(q_ref, k_ref, v_ref, dl_ref, s0_ref, *rest, C, n_a, per_a, per_b, n_chunks,
                backward):
    if backward:
        of_ref, rg_ref, gn_ref, y_ref, sfin_ref, S, dmat, qd, kd = rest
        c = n_chunks - 1 - pl.program_id(0)
    else:
        o_ref, sfin_ref, S, dmat, qd, kd = rest
        c = pl.program_id(0)
    _, pos, n = _seq_pos(c, n_a, per_a, per_b)
    first = (pos == n - 1) if backward else (pos == 0)
    last = (pos == 0) if backward else (pos == n - 1)
    H, dk, dv = RET_H, RET_DK, RET_DV

    @pl.when(pl.program_id(0) == 0)
    def _():
        row = lax.broadcasted_iota(jnp.int32, (C, LANES), 0).astype(F32)
        col = lax.broadcasted_iota(jnp.int32, (C, LANES), 1).astype(F32)
        for h in range(H):
            lg = _log_sigmoid(dl_ref[h])[0:1, :]
            if backward:
                qd[h] = jnp.exp((C - row) * lg)
                kd[h] = jnp.exp(row * lg)
            else:
                qd[h] = jnp.exp((row + 1.0) * lg)
                kd[h] = jnp.exp((C - 1.0 - row) * lg)
            for cb in range(C // LANES):
                diff = row - (col + cb * LANES)
                if backward:
                    diff = -diff
                dm = jnp.where(diff >= 0, jnp.exp(jnp.maximum(diff, 0.0) * lg), 0.0)
                dmat[h, :, cb * LANES:(cb + 1) * LANES] = dm

    @pl.when(first)
    def _():
        S[...] = s0_ref[...]

    nt = (((1,), (1,)), ((), ()))
    tn = (((0,), (0,)), ((), ()))
    for h in range(H):
        q = q_ref[:, h * dk:(h + 1) * dk].astype(BF16)
        kf = k_ref[:, h * dk:(h + 1) * dk] * (dk ** -0.5)
        v = v_ref[:, h * dv:(h + 1) * dv].astype(BF16)
        lg = _log_sigmoid(dl_ref[h])[0:1, :]
        att = lax.dot_general(q, kf.astype(BF16), nt, preferred_element_type=F32) * dmat[h]
        s_old = S[h]
        o = jnp.dot(att.astype(BF16), v, preferred_element_type=F32)
        o = o + qd[h] * jnp.dot(q, s_old.astype(BF16), preferred_element_type=F32)
        kv = lax.dot_general((kf * kd[h]).astype(BF16), v, tn, preferred_element_type=F32)
        S[h] = jnp.exp(C * lg) * s_old + kv
        if backward:
            t = of_ref[:, h * dv:(h + 1) * dv] + o
            mu = jnp.mean(t, axis=-1, keepdims=True)
            d = t - mu
            var = jnp.mean(d * d, axis=-1, keepdims=True)
            yn = d * lax.rsqrt(var + EPS) * gn_ref[:, h * dv:(h + 1) * dv]
            y_ref[:, h * dv:(h + 1) * dv] = (yn * _silu(rg_ref[:, h * dv:(h + 1) * dv])).astype(
                y_ref.dtype)
        else:
            o_ref[:, h * dv:(h + 1) * dv] = o

    @pl.when(last)
    def _():
        sfin_ref[...] = S[...]


def _retention(z, decay_logit, gn_g, s0_all, n_a_tok, len_a, len_b, C=256):
    tok = z.shape[0]
    H, dk, dv = RET_H, RET_DK, RET_DV
    W = H * dk
    n_chunks = tok // C
    n_a, per_a, per_b = n_a_tok // C, len_a // C, len_b // C
    n_seq = s0_all.shape[0]
    dl = jnp.broadcast_to(decay_logit[:, :, None, None], (2, H, 8, LANES)).astype(F32)
    geo = dict(C=C, n_a=n_a, per_a=per_a, per_b=per_b, n_chunks=n_chunks)

    def order(backward):
        return (lambda g: n_chunks - 1 - g) if backward else (lambda g: g)

    def zspec(col0, backward):
        cidx = col0 // W
        f = order(backward)
        return pl.BlockSpec((C, W), lambda g: (f(g), cidx))

    def seq_of(backward):
        f = order(backward)
        return lambda g: _seq_pos(f(g), n_a, per_a, per_b)[0]

    def common_in(d):
        b = d == 1
        s = seq_of(b)
        return [
            zspec(ZC_RQ, b), zspec(ZC_RK, b), zspec(ZC_RV, b),
            pl.BlockSpec((None, H, 8, LANES), lambda g: (d, 0, 0, 0)),
            pl.BlockSpec((None, None, H, dk, dv), lambda g: (s(g), d, 0, 0, 0)),
        ]

    def sfin_spec(d):
        s = seq_of(d == 1)
        return pl.BlockSpec((None, H, dk, dv), lambda g: (s(g), 0, 0, 0))

    scratch = [pltpu.VMEM((H, dk, dv), F32), pltpu.VMEM((H, C, C), F32),
               pltpu.VMEM((H, C, LANES), F32), pltpu.VMEM((H, C, LANES), F32)]
    sfin_shape = jax.ShapeDtypeStruct((n_seq, H, dk, dv), F32)

    o_f, s_f = pl.pallas_call(
        functools.partial(_ret_kernel, backward=False, **geo),
        out_shape=(jax.ShapeDtypeStruct((tok, W), F32), sfin_shape),
        grid=(n_chunks,),
        in_specs=common_in(0),
        out_specs=(pl.BlockSpec((C, W), lambda g: (g, 0)), sfin_spec(0)),
        scratch_shapes=scratch,
        compiler_params=_params("arbitrary"),
        name="retention_fwd",
    )(z, z, z, dl, s0_all)

    rev = order(True)
    y, s_b = pl.pallas_call(
        functools.partial(_ret_kernel, backward=True, **geo),
        out_shape=(jax.ShapeDtypeStruct((tok, W), BF16), sfin_shape),
        grid=(n_chunks,),
        in_specs=common_in(1) + [
            pl.BlockSpec((C, W), lambda g: (rev(g), 0)),
            zspec(ZC_RG, True),
            pl.BlockSpec((1, W), lambda g: (0, 0)),
        ],
        out_specs=(pl.BlockSpec((C, W), lambda g: (rev(g), 0)), sfin_spec(1)),
        scratch_shapes=scratch,
        compiler_params=_params("arbitrary"),
        name="retention_bwd",
    )(z, z, z, dl, s0_all, o_f, z, gn_g.reshape(1, W))
    return y, jnp.stack([s_f, s_b], axis=1)


def _rope(x, cos, sin):
    lane = lax.broadcasted_iota(jnp.int32, x.shape, 1)
    fwd = pltpu.roll(x, LANES - 16, axis=1)
    bwd = pltpu.roll(x, 16, axis=1)
    partner = jnp.where((lane % 32) < 16, fwd, bwd)
    return x * cos + partner * sin


def _mla_prep_kernel(cq_ref, ckv_ref, kr_ref, cos_ref, sin_ref, qg_ref, kg_ref, wq_ref, wk_ref,
                     wv_ref, q_out, k_out, v_out, ckvn_out):
    cos = cos_ref[...]
    sin = sin_ref[...]
    cqn = _rms(cq_ref[...], qg_ref[...]).astype(BF16)
    qa = jnp.dot(cqn, wq_ref[...], preferred_element_type=F32)
    ckvn = _rms(ckv_ref[...], kg_ref[...])
    ckvn_out[...] = ckvn
    kb = ckvn.astype(BF16)
    kn = jnp.dot(kb, wk_ref[...], preferred_element_type=F32)
    v_out[...] = jnp.dot(kb, wv_ref[...], preferred_element_type=F32).astype(BF16)
    kr = _rope(kr_ref[...], cos, sin).astype(BF16)
    for h in range(MLA_H):
        c0 = h * QK_PAD
        q_out[:, c0:c0 + LANES] = qa[:, c0:c0 + LANES].astype(BF16)
        q_out[:, c0 + LANES:c0 + QK_PAD] = _rope(qa[:, c0 + LANES:c0 + QK_PAD], cos, sin).astype(
            BF16)
        k_out[:, c0:c0 + LANES] = kn[:, h * MLA_DN:(h + 1) * MLA_DN].astype(BF16)
        k_out[:, c0 + LANES:c0 + QK_PAD] = kr


def _ctx_prep_kernel(ckv_ref, kr_ref, wk_ref, wv_ref, k_out, v_out):
    kb = ckv_ref[...].astype(BF16)
    kn = jnp.dot(kb, wk_ref[...], preferred_element_type=F32)
    v_out[...] = jnp.dot(kb, wv_ref[...], preferred_element_type=F32).astype(BF16)
    kr = kr_ref[...].astype(BF16)
    for h in range(MLA_H):
        c0 = h * QK_PAD
        k_out[:, c0:c0 + LANES] = kn[:, h * MLA_DN:(h + 1) * MLA_DN].astype(BF16)
        k_out[:, c0 + LANES:c0 + QK_PAD] = kr


def _mla_weights(w_uq, w_ukv):
    H, dn, dr, dv = MLA_H, MLA_DN, MLA_DR, MLA_DV
    wq = w_uq.reshape(-1, H, dn + dr)
    wq = jnp.concatenate([wq, jnp.zeros(wq.shape[:2] + (QK_PAD - dn - dr,), wq.dtype)], axis=-1)
    wkv = w_ukv.reshape(-1, H, dn + dv)
    wk = wkv[..., :dn].reshape(-1, H * dn)
    wv = wkv[..., dn:].reshape(-1, H * dv)
    return wq.reshape(-1, H * QK_PAD).astype(BF16), wk.astype(BF16), wv.astype(BF16)


def _mla_prep(z, rope_cos, rope_sin, q_norm, kv_norm, wq, wk, wv, seg, tm=512):
    tok = z.shape[0]
    H = MLA_H
    per = seg // tm
    tab = pl.BlockSpec((None, tm, LANES), lambda i: (jnp.minimum(i // per, 1), i % per, 0))
    full = lambda a: pl.BlockSpec(a.shape, lambda i: (0,) * a.ndim)
    qg = q_norm.reshape(1, -1)
    kg = kv_norm.reshape(1, -1)
    return pl.pallas_call(
        _mla_prep_kernel,
        out_shape=(jax.ShapeDtypeStruct((tok, H * QK_PAD), BF16),
                   jax.ShapeDtypeStruct((tok, H * QK_PAD), BF16),
                   jax.ShapeDtypeStruct((tok, H * MLA_DV), BF16),
                   jax.ShapeDtypeStruct((tok, MLA_KV_LORA), F32)),
        grid=(tok // tm,),
        in_specs=[
            pl.BlockSpec((tm, MLA_Q_LORA), lambda i: (i, ZC_CQ // MLA_Q_LORA)),
            pl.BlockSpec((tm, MLA_KV_LORA), lambda i: (i, ZC_CKV // MLA_KV_LORA)),
            pl.BlockSpec((tm, LANES), lambda i: (i, ZC_KR // LANES)),
            tab, tab, full(qg), full(kg), full(wq), full(wk), full(wv),
        ],
        out_specs=(pl.BlockSpec((tm, H * QK_PAD), lambda i: (i, 0)),
                   pl.BlockSpec((tm, H * QK_PAD), lambda i: (i, 0)),
                   pl.BlockSpec((tm, H * MLA_DV), lambda i: (i, 0)),
                   pl.BlockSpec((tm, MLA_KV_LORA), lambda i: (i, 0))),
        compiler_params=_params("parallel"),
        name="mla_prep",
    )(z, z, z, rope_cos, rope_sin, qg, kg, wq, wk, wv)


def _ctx_prep(ckv_ctx, kpe_ctx, wk, wv, tm=256):
    rows = ckv_ctx.shape[0]
    H = MLA_H
    full = lambda a: pl.BlockSpec(a.shape, lambda i: (0,) * a.ndim)
    return pl.pallas_call(
        _ctx_prep_kernel,
        out_shape=(jax.ShapeDtypeStruct((rows, H * QK_PAD), BF16),
                   jax.ShapeDtypeStruct((rows, H * MLA_DV), BF16)),
        grid=(rows // tm,),
        in_specs=[pl.BlockSpec((tm, MLA_KV_LORA), lambda i: (i, 0)),
                  pl.BlockSpec((tm, LANES), lambda i: (i, 0)), full(wk), full(wv)],
        out_specs=(pl.BlockSpec((tm, H * QK_PAD), lambda i: (i, 0)),
                   pl.BlockSpec((tm, H * MLA_DV), lambda i: (i, 0))),
        compiler_params=_params("parallel"),
        name="ctx_prep",
    )(ckv_ctx, kpe_ctx, wk, wv)


def _attn_kernel(*refs, has_ctx):
    if has_ctx:
        q_ref, k_ref, v_ref, kc_ref, vc_ref, o_ref = refs
    else:
        q_ref, k_ref, v_ref, o_ref = refs
    nt = (((1,), (1,)), ((), ()))
    q = q_ref[...]
    s = lax.dot_general(q, k_ref[...], nt, preferred_element_type=F32) * MLA_SCALE
    m = jnp.max(s, axis=-1, keepdims=True)
    if has_ctx:
        sc = lax.dot_general(q, kc_ref[...], nt, preferred_element_type=F32) * MLA_SCALE
        m = jnp.maximum(m, jnp.max(sc, axis=-1, keepdims=True))
    p = jnp.exp(s - m)
    den = jnp.sum(p, axis=-1, keepdims=True)
    o = jnp.dot(p.astype(BF16), v_ref[...], preferred_element_type=F32)
    if has_ctx:
        pc = jnp.exp(sc - m)
        den = den + jnp.sum(pc, axis=-1, keepdims=True)
        o = o + jnp.dot(pc.astype(BF16), vc_ref[...], preferred_element_type=F32)
    o_ref[...] = (o / den).astype(o_ref.dtype)


def _attention(qp, kp, v, tok0, B, T, ctx=None, tq=256):
    H, dv = MLA_H, MLA_DV
    s0 = tok0 // T
    q0 = tok0 // tq
    nq = T // tq
    in_specs = [
        pl.BlockSpec((tq, QK_PAD), lambda b, h, i: (q0 + b * nq + i, h)),
        pl.BlockSpec((T, QK_PAD), lambda b, h, i: (s0 + b, h)),
        pl.BlockSpec((T, dv), lambda b, h, i: (s0 + b, h)),
    ]
    args = [qp, kp, v]
    if ctx is not None:
        past = ctx[0].shape[0] // B
        in_specs += [pl.BlockSpec((past, QK_PAD), lambda b, h, i: (b, h)),
                     pl.BlockSpec((past, dv), lambda b, h, i: (b, h))]
        args += list(ctx)
    return pl.pallas_call(
        functools.partial(_attn_kernel, has_ctx=ctx is not None),
        out_shape=jax.ShapeDtypeStruct((B * T, H * dv), BF16),
        grid=(B, H, nq),
        in_specs=in_specs,
        out_specs=pl.BlockSpec((tq, dv), lambda b, h, i: (b * nq + i, h)),
        compiler_params=_params("parallel", "parallel", "arbitrary"),
        name="attention_ctx" if ctx is not None else "attention",
    )(*args)


def _route(h, wr_ref, br_ref):
    logits = jnp.dot(h, wr_ref[...], preferred_element_type=F32,
                     precision=lax.Precision.HIGHEST) + br_ref[...]
    lane = lax.broadcasted_iota(jnp.int32, logits.shape, 1).astype(F32)
    neg = jnp.float32(-jnp.inf)
    logits = jnp.where(lane < N_EXPERTS, logits, neg)
    m1 = jnp.max(logits, axis=-1, keepdims=True)
    i1 = jnp.min(jnp.where(logits == m1, lane, float(LANES)), axis=-1, keepdims=True)
    rest = jnp.where(lane == i1, neg, logits)
    m2 = jnp.max(rest, axis=-1, keepdims=True)
    i2 = jnp.min(jnp.where(rest == m2, lane, float(LANES)), axis=-1, keepdims=True)
    e = jnp.exp(m2 - m1)
    g1 = 1.0 / (1.0 + e)
    g2 = e / (1.0 + e)
    rec = jnp.where(lane == 0, i1, 0.0)
    rec = jnp.where(lane == 1, i2, rec)
    rec = jnp.where(lane == 2, g1, rec)
    return jnp.where(lane == 3, g2, rec)


def _outproj_kernel(yc_ref, yr_ref, ym_ref, w_ref, x_ref, gate_ref, n1_ref, n2_ref, sc_ref,
                    sh_ref, *rest, routed):
    if routed:
        wr_ref, br_ref, x1_ref, h_ref, r_ref = rest
    else:
        x1_ref, h_ref = rest
    c1 = yc_ref.shape[1]
    c2 = c1 + yr_ref.shape[1]
    y = jnp.dot(yc_ref[...], w_ref[0:c1, :], preferred_element_type=F32)
    y = y + jnp.dot(yr_ref[...], w_ref[c1:c2, :], preferred_element_type=F32)
    y = y + jnp.dot(ym_ref[...], w_ref[c2:, :], preferred_element_type=F32)
    x1 = x_ref[...] + gate_ref[...] * _rms(y, n1_ref[...])
    x1_ref[...] = x1
    h = _rms(x1, n2_ref[...]) * (1.0 + sc_ref[...]) + sh_ref[...]
    h_ref[...] = h.astype(h_ref.dtype)
    if routed:
        r_ref[...] = _route(h, wr_ref, br_ref)


def _outproj(yc, yr, ym, w_out, x, gains, mod, seg, router=None, tm=512):
    tok, D = x.shape
    routed = router is not None
    row = lambda a: pl.BlockSpec((tm, a.shape[1]), lambda i: (i, 0))
    full = lambda a: pl.BlockSpec(a.shape, lambda i: (0,) * a.ndim)
    in_specs = [row(yc), row(yr), row(ym), full(w_out), row(x), _mod_spec(2, tm, seg),
                _gain_spec(1), _gain_spec(2), _mod_spec(4, tm, seg), _mod_spec(3, tm, seg)]
    args = [yc, yr, ym, w_out, x, mod, gains, gains, mod, mod]
    out_shape = [jax.ShapeDtypeStruct((tok, D), F32),
                 jax.ShapeDtypeStruct((tok, D), F32 if routed else BF16)]
    out_specs = [pl.BlockSpec((tm, D), lambda i: (i, 0)), pl.BlockSpec((tm, D), lambda i: (i, 0))]
    if routed:
        in_specs += [full(router[0]), full(router[1])]
        args += list(router)
        out_shape.append(jax.ShapeDtypeStruct((tok, LANES), F32))
        out_specs.append(pl.BlockSpec((tm, LANES), lambda i: (i, 0)))
    return pl.pallas_call(
        functools.partial(_outproj_kernel, routed=routed),
        out_shape=tuple(out_shape),
        grid=(tok // tm,),
        in_specs=in_specs,
        out_specs=tuple(out_specs),
        compiler_params=_params("parallel"),
        name="outproj_routed" if routed else "outproj",
    )(*args)


def _ffn_kernel(h_ref, wg_ref, wu_ref, wd_ref, x_ref, gate_ref, n_ref, o_ref, acc):
    f = pl.program_id(1)

    @pl.when(f == 0)
    def _():
        acc[...] = jnp.zeros_like(acc)

    h = h_ref[...]
    g = jnp.dot(h, wg_ref[...], preferred_element_type=F32)
    u = jnp.dot(h, wu_ref[...], preferred_element_type=F32)
    acc[...] += jnp.dot((_silu(g) * u).astype(BF16), wd_ref[...], preferred_element_type=F32)

    @pl.when(f == pl.num_programs(1) - 1)
    def _():
        o_ref[...] = x_ref[...] + gate_ref[...] * _rms(acc[...], n_ref[...])


def _dense_ffn(h, wg, wu, wd, x, gains, mod, seg, tm=512, tf=512):
    tok, D = x.shape
    F = wg.shape[1]
    return pl.pallas_call(
        _ffn_kernel,
        out_shape=jax.ShapeDtypeStruct((tok, D), F32),
        grid=(tok // tm, F // tf),
        in_specs=[
            pl.BlockSpec((tm, D), lambda i, f: (i, 0)),
            pl.BlockSpec((D, tf), lambda i, f: (0, f)),
            pl.BlockSpec((D, tf), lambda i, f: (0, f)),
            pl.BlockSpec((tf, D), lambda i, f: (f, 0)),
            pl.BlockSpec((tm, D), lambda i, f: (i, 0)),
            _mod_spec(5, tm, seg),
            _gain_spec(3),
        ],
        out_specs=pl.BlockSpec((tm, D), lambda i, f: (i, 0)),
        scratch_shapes=[pltpu.VMEM((tm, D), F32)],
        compiler_params=_params("parallel", "arbitrary"),
        name="dense_ffn",
    )(h, wg, wu, wd, x, mod, gains)


def _gather_rows_kernel(idx_ref, src_ref, dst_ref, sem, *, rows):
    base = pl.program_id(0) * rows

    def copy(r):
        return pltpu.make_async_copy(src_ref.at[pl.ds(idx_ref[base + r], 1), :],
                                     dst_ref.at[pl.ds(base + r, 1), :], sem)

    def start(r, carry):
        copy(r).start()
        return carry

    def wait(r, carry):
        pltpu.make_async_copy(src_ref.at[pl.ds(0, 1), :], dst_ref.at[pl.ds(base + r, 1), :],
                              sem).wait()
        return carry

    lax.fori_loop(0, rows, start, 0)
    lax.fori_loop(0, rows, wait, 0)


def _gather_rows(src, idx, rows=512):
    n = idx.shape[0]
    return pl.pallas_call(
        functools.partial(_gather_rows_kernel, rows=rows),
        out_shape=jax.ShapeDtypeStruct((n, src.shape[1]), src.dtype),
        grid_spec=pltpu.PrefetchScalarGridSpec(
            num_scalar_prefetch=1,
            grid=(n // rows,),
            in_specs=[pl.BlockSpec(memory_space=pl.ANY)],
            out_specs=pl.BlockSpec(memory_space=pl.ANY),
            scratch_shapes=[pltpu.SemaphoreType.DMA],
        ),
        compiler_params=_params("arbitrary"),
        name="moe_gather",
    )(idx, src)


def _moe_ffn_kernel(te_ref, nu_ref, xs_ref, wg_ref, wu_ref, wd_ref, o_ref, xb, acc):
    i = pl.program_id(0)
    f = pl.program_id(1)
    nf = pl.num_programs(1)
    used = i < nu_ref[0]

    @pl.when(used & (f == 0))
    def _():
        xb[...] = xs_ref[...].astype(BF16)
        acc[...] = jnp.zeros_like(acc)

    @pl.when(used)
    def _():
        x = xb[...]
        g = jnp.dot(x, wg_ref[...], preferred_element_type=F32)
        u = jnp.dot(x, wu_ref[...], preferred_element_type=F32)
        acc[...] += jnp.dot((_silu(g) * u).astype(BF16), wd_ref[...], preferred_element_type=F32)

    @pl.when(used & (f == nf - 1))
    def _():
        o_ref[...] = acc[...]

    @pl.when(jnp.logical_not(used) & (f == nf - 1))
    def _():
        o_ref[...] = jnp.zeros_like(o_ref)


def _moe_ffn(xs, tile_expert, n_used, wg, wu, wd, tm, tf=512):
    n_slots, D = xs.shape
    F = wg.shape[2]
    nf = F // tf
    n_tiles = n_slots // tm

    def fi(i, f, nu):
        return jnp.where(i < nu[0], f, nf - 1)

    return pl.pallas_call(
        _moe_ffn_kernel,
        out_shape=jax.ShapeDtypeStruct((n_slots, D), F32),
        grid_spec=pltpu.PrefetchScalarGridSpec(
            num_scalar_prefetch=2,
            grid=(n_tiles, nf),
            in_specs=[
                pl.BlockSpec((tm, D), lambda i, f, te, nu: (jnp.minimum(i, nu[0] - 1), 0)),
                pl.BlockSpec((None, D, tf), lambda i, f, te, nu: (te[i], 0, fi(i, f, nu))),
                pl.BlockSpec((None, D, tf), lambda i, f, te, nu: (te[i], 0, fi(i, f, nu))),
                pl.BlockSpec((None, tf, D), lambda i, f, te, nu: (te[i], fi(i, f, nu), 0)),
            ],
            out_specs=pl.BlockSpec((tm, D), lambda i, f, te, nu: (i, 0)),
            scratch_shapes=[pltpu.VMEM((tm, D), BF16), pltpu.VMEM((tm, D), F32)],
        ),
        compiler_params=_params("arbitrary", "arbitrary"),
        name="moe_ffn",
    )(tile_expert, n_used, xs, wg, wu, wd)


def _combine_kernel(d_ref, ys_ref, gates_ref, x_ref, gate_ref, n_ref, o_ref, b0, b1, sem,
                    *, rows):
    base = pl.program_id(0) * rows

    def copies(r):
        a = 2 * (base + r)
        return (pltpu.make_async_copy(ys_ref.at[pl.ds(d_ref[a], 1), :],
                                      b0.at[pl.ds(r, 1), :], sem.at[0]),
                pltpu.make_async_copy(ys_ref.at[pl.ds(d_ref[a + 1], 1), :],
                                      b1.at[pl.ds(r, 1), :], sem.at[1]))

    def start(r, carry):
        for cp in copies(r):
            cp.start()
        return carry

    def wait(r, carry):
        for cp in copies(r):
            cp.wait()
        return carry

    lax.fori_loop(0, rows, start, 0)
    lax.fori_loop(0, rows, wait, 0)
    gates = gates_ref[...]
    out = gates[:, 2:3] * b0[...] + gates[:, 3:4] * b1[...]
    o_ref[...] = x_ref[...] + gate_ref[...] * _rms(out, n_ref[...])


def _moe_combine(ys, dest, route, x, gains, mod, seg, rows=256):
    tok, D = x.shape
    return pl.pallas_call(
        functools.partial(_combine_kernel, rows=rows),
        out_shape=jax.ShapeDtypeStruct((tok, D), F32),
        grid_spec=pltpu.PrefetchScalarGridSpec(
            num_scalar_prefetch=1,
            grid=(tok // rows,),
            in_specs=[
                pl.BlockSpec(memory_space=pl.ANY),
                pl.BlockSpec((rows, LANES), lambda i, d: (i, 0)),
                pl.BlockSpec((rows, D), lambda i, d: (i, 0)),
                _mod_spec(5, rows, seg),
                _gain_spec(3),
            ],
            out_specs=pl.BlockSpec((rows, D), lambda i, d: (i, 0)),
            scratch_shapes=[pltpu.VMEM((rows, D), F32), pltpu.VMEM((rows, D), F32),
                            pltpu.SemaphoreType.DMA((2,))],
        ),
        compiler_params=_params("arbitrary"),
        name="moe_combine",
    )(dest, ys, route, x, mod, gains)


def _moe_plan(route, tm):
    tok = route.shape[0]
    n_assign = tok * TOP_K
    n_tiles = -(-n_assign // tm) + N_EXPERTS
    n_slots = n_tiles * tm
    expert = route[:, :TOP_K].astype(jnp.int32).reshape(-1)
    token = jnp.repeat(jnp.arange(tok, dtype=jnp.int32), TOP_K)
    onehot = (expert[:, None] == jnp.arange(N_EXPERTS, dtype=jnp.int32)[None, :]).astype(jnp.int32)
    csum = jnp.cumsum(onehot, axis=0)
    counts = csum[-1]
    padded = (counts + tm - 1) // tm * tm
    padded_end = jnp.cumsum(padded)
    padded_start = padded_end - padded
    dest = jnp.sum(onehot * (csum - 1 + padded_start[None, :]), axis=1).astype(jnp.int32)
    slot_token = jnp.zeros((n_slots,), jnp.int32).at[dest].set(token)
    n_used = (padded_end[-1] // tm).astype(jnp.int32)
    tile_start = jnp.minimum(jnp.arange(n_tiles, dtype=jnp.int32), n_used - 1) * tm
    tile_expert = jnp.minimum(jnp.searchsorted(padded_end, tile_start, side='right'),
                              N_EXPERTS - 1).astype(jnp.int32)
    return slot_token, dest, tile_expert, n_used.reshape(1)


def _moe_layer(h, route, x, wg, wu, wd, gains, mod, seg, tm=512):
    slot_token, dest, tile_expert, n_used = _moe_plan(route, tm)
    xs = _gather_rows(h, slot_token)
    ys = _moe_ffn(xs, tile_expert, n_used, wg, wu, wd, tm)
    return _moe_combine(ys, dest, route, x, gains, mod, seg)


def _rope_tables(n_tokens):
    half = ROPE_AXIS // 2
    t = jnp.arange(n_tokens)
    row = (t // GRID_W).astype(F32)
    col = (t % GRID_W).astype(F32)
    inv_freq = jnp.power(ROPE_BASE, -jnp.arange(0, ROPE_AXIS, 2, dtype=F32) / ROPE_AXIS)
    ar = row[:, None] * inv_freq
    ac = col[:, None] * inv_freq
    pad = LANES - MLA_DR
    cos = jnp.concatenate([jnp.cos(ar), jnp.cos(ar), jnp.cos(ac), jnp.cos(ac),
                           jnp.ones((n_tokens, pad), F32)], axis=1)
    sin = jnp.concatenate([-jnp.sin(ar), jnp.sin(ar), -jnp.sin(ac), jnp.sin(ac),
                           jnp.zeros((n_tokens, pad), F32)], axis=1)
    assert cos.shape[1] == LANES and half * 4 == MLA_DR
    ident = (jnp.ones_like(cos), jnp.zeros_like(sin))
    return jnp.stack([ident[0], cos]), jnp.stack([ident[1], sin])


def kernel(x_prompt, x_sample, cache_mla_ckv, cache_mla_kpe, state_ret, c, c_ctx, w_mod, b_mod, norm_gains, w_in, w_out, conv_w, conv_b, conv_ln_g, conv_ln_b, ret_decay_logit, ret_gn_g, mla_q_norm, mla_w_uq, mla_kv_norm, mla_w_ukv, ffn_w_gate, ffn_w_up, ffn_w_down, moe_w_router, moe_b_router, moe_w_gate, moe_w_up, moe_w_down):
    D = D_MODEL
    n_p = BATCH * SEQ
    n_s = DEC_BATCH * DEC_SEQ
    seg = DEC_SEQ
    assert n_p == seg, "context tokens must fill exactly one conditioning segment"
    n_seg = 1 + DEC_BATCH

    x = jnp.concatenate([x_prompt.reshape(n_p, D), x_sample.reshape(n_s, D)], axis=0)
    cond8 = jnp.concatenate([c_ctx[None, :], c, jnp.zeros((8 - n_seg, D), F32)], axis=0)
    mod = _modulation(cond8, w_mod, b_mod).reshape(DEPTH, 8, 6, 1, D)
    gains = norm_gains.reshape(DEPTH, 4, 1, D)
    rope_cos, rope_sin = _rope_tables(DEC_SEQ)
    kpe_pad = jnp.concatenate(
        [cache_mla_kpe, jnp.zeros(cache_mla_kpe.shape[:-1] + (LANES - MLA_DR,), F32)], axis=-1)

    ckv_layers, kpe_layers, ret_layers = [], [], []
    for l in range(DEPTH):
        w_in_p = jnp.concatenate([w_in[l], jnp.zeros((D, ZW - IN_COLS), F32)], axis=1).astype(BF16)
        z = _inproj(x, gains[l], mod[l], w_in_p, seg)

        y_conv = _conv_module(z, conv_w[l], conv_b[l], conv_ln_g[l], conv_ln_b[l], n_p, SEQ,
                              DEC_SEQ)

        s0_all = jnp.concatenate(
            [jnp.zeros((BATCH, 2, RET_H, RET_DK, RET_DV), F32), state_ret[:, l]], axis=0)
        y_ret, s_fin = _retention(z, ret_decay_logit[l], ret_gn_g[l], s0_all, n_p, SEQ, DEC_SEQ)

        wq, wk, wv = _mla_weights(mla_w_uq[l], mla_w_ukv[l])
        qp, kp, v, ckv_n = _mla_prep(z, rope_cos, rope_sin, mla_q_norm[l], mla_kv_norm[l],
                                     wq, wk, wv, seg)
        k_ctx, v_ctx = _ctx_prep(cache_mla_ckv[:, l].reshape(DEC_BATCH * PAST_LEN, MLA_KV_LORA),
                                 kpe_pad[:, l].reshape(DEC_BATCH * PAST_LEN, LANES), wk, wv)
        y_mla = jnp.concatenate([
            _attention(qp, kp, v, 0, BATCH, SEQ),
            _attention(qp, kp, v, n_p, DEC_BATCH, DEC_SEQ, ctx=(k_ctx, v_ctx)),
        ], axis=0)

        ckv_layers.append(ckv_n[:n_p].reshape(BATCH, SEQ, MLA_KV_LORA))
        kpe_layers.append(z[:n_p, ZC_KR:ZC_KR + MLA_DR].reshape(BATCH, SEQ, MLA_DR))
        ret_layers.append(s_fin[:BATCH])

        w_out_b = w_out[l].astype(BF16)
        i = l // 2
        if l % 2 == 0:
            x1, h = _outproj(y_conv, y_ret, y_mla, w_out_b, x, gains[l], mod[l], seg)
            x = _dense_ffn(h, ffn_w_gate[i].astype(BF16), ffn_w_up[i].astype(BF16),
                           ffn_w_down[i].astype(BF16), x1, gains[l], mod[l], seg)
        else:
            wr = jnp.concatenate([moe_w_router[i], jnp.zeros((D, LANES - N_EXPERTS), F32)], axis=1)
            br = jnp.concatenate([moe_b_router[i], jnp.zeros((LANES - N_EXPERTS,), F32)])[None, :]
            x1, h, route = _outproj(y_conv, y_ret, y_mla, w_out_b, x, gains[l], mod[l], seg,
                                    router=(wr, br))
            x = _moe_layer(h, route, x1, moe_w_gate[i].astype(BF16), moe_w_up[i].astype(BF16),
                           moe_w_down[i].astype(BF16), gains[l], mod[l], seg)

    y_prompt = x[:n_p].reshape(BATCH, SEQ, D)
    y_sample = x[n_p:].reshape(DEC_BATCH, DEC_SEQ, D)
    return (y_prompt, y_sample, jnp.stack(ckv_layers, axis=1), jnp.stack(kpe_layers, axis=1),
            jnp.stack(ret_layers, axis=1))
```

```python
import functools

import jax
import jax.numpy as jnp
from jax import lax
from jax.experimental import pallas as pl
from jax.experimental.pallas import tpu as pltpu

F32 = jnp.float32
BF16 = jnp.bfloat16

D_MODEL = 2048
BATCH = 16
SEQ = 256
DEPTH = 2
DEC_BATCH = 4
DEC_SEQ = 4096
PAST_LEN = 256
GRID_W = 64
EPS = 1e-6
CONV_C = 512
CONV_K = 31
RET_H = 4
RET_DK = 128
RET_DV = 128
MLA_H = 8
MLA_DN = 128
MLA_DR = 64
MLA_DV = 128
MLA_Q_LORA = 768
MLA_KV_LORA = 256
MLA_SCALE = (MLA_DN + MLA_DR) ** -0.5
LOG2E = 1.4426950408889634
ROPE_BASE = 10000.0
ROPE_AXIS = MLA_DR // 2
D_FF = 5632
N_EXPERTS = 8
TOP_K = 2
D_EXPERT = 7168

LANES = 128
HALO = 16
QK_PAD = 256

ZC_CONV = 0
ZC_RQ = 2 * CONV_C
ZC_RK = ZC_RQ + RET_H * RET_DK
ZC_RV = ZC_RK + RET_H * RET_DK
ZC_RG = ZC_RV + RET_H * RET_DV
ZC_CQ = ZC_RG + RET_H * RET_DV
ZC_CKV = ZC_CQ + MLA_Q_LORA
ZC_KR = ZC_CKV + MLA_KV_LORA
IN_COLS = ZC_KR + MLA_DR
ZW = 4608


def _params(*sem):
    return pltpu.CompilerParams(dimension_semantics=sem)


def _silu(x):
    return x * jax.nn.sigmoid(x)


def _rms(x, g):
    return x * lax.rsqrt(jnp.mean(x * x, axis=-1, keepdims=True) + EPS) * g


def _mod_kernel(c_ref, w_ref, b_ref, o_ref):
    c = c_ref[...]
    a = _silu(c).astype(BF16)
    o_ref[...] = jnp.dot(a, w_ref[...].astype(BF16), preferred_element_type=F32) + b_ref[...]


def _modulation(cond8, w_mod, b_mod, tn=1024):
    L, D, N = w_mod.shape
    return pl.pallas_call(
        _mod_kernel,
        out_shape=jax.ShapeDtypeStruct((L, 8, N), F32),
        grid=(L, N // tn),
        in_specs=[
            pl.BlockSpec((8, D), lambda l, j: (0, 0)),
            pl.BlockSpec((None, D, tn), lambda l, j: (l, 0, j)),
            pl.BlockSpec((None, 1, tn), lambda l, j: (l, 0, j)),
        ],
        out_specs=pl.BlockSpec((None, 8, tn), lambda l, j: (l, 0, j)),
        compiler_params=_params("parallel", "arbitrary"),
        name="modulation",
    )(cond8, w_mod, b_mod.reshape(L, 1, N))


def _inproj_kernel(x_ref, g_ref, sc_ref, sh_ref, w_ref, o_ref, h_scr):
    @pl.when(pl.program_id(1) == 0)
    def _():
        h = _rms(x_ref[...], g_ref[...]) * (1.0 + sc_ref[...]) + sh_ref[...]
        h_scr[...] = h.astype(BF16)

    o_ref[...] = jnp.dot(h_scr[...], w_ref[...], preferred_element_type=F32)


def _mod_spec(chunk, tm, seg):
    return pl.BlockSpec((None, None, 1, D_MODEL), lambda i, *_: (i // (seg // tm), chunk, 0, 0))


def _gain_spec(k):
    return pl.BlockSpec((None, 1, D_MODEL), lambda i, *_: (k, 0, 0))


def _inproj(x, gains, mod, w_in_p, seg, tm=1024, tn=512):
    tok, D = x.shape
    zw = w_in_p.shape[1]
    return pl.pallas_call(
        _inproj_kernel,
        out_shape=jax.ShapeDtypeStruct((tok, zw), F32),
        grid=(tok // tm, zw // tn),
        in_specs=[
            pl.BlockSpec((tm, D), lambda i, j: (i, 0)),
            _gain_spec(0),
            _mod_spec(1, tm, seg),
            _mod_spec(0, tm, seg),
            pl.BlockSpec((D, tn), lambda i, j: (0, j)),
        ],
        out_specs=pl.BlockSpec((tm, tn), lambda i, j: (i, j)),
        scratch_shapes=[pltpu.VMEM((tm, D), BF16)],
        compiler_params=_params("parallel", "arbitrary"),
        name="inproj",
    )(x, gains, mod, mod, w_in_p)


def _seq_pos(t, n_a, per_a, per_b):
    in_a = t < n_a
    tb = t - n_a
    seq = jnp.where(in_a, t // per_a, n_a // per_a + tb // per_b)
    pos = jnp.where(in_a, t % per_a, tb % per_b)
    n = jnp.where(in_a, per_a, per_b)
    return seq, pos, n


def _conv_kernel(zc_ref, zp_ref, zn_ref, w_ref, b_ref, g_ref, bb_ref, o_ref, ubuf, cbuf,
                 *, tt, n_a, per_a, per_b, rows):
    _, pos, n = _seq_pos(pl.program_id(0), n_a, per_a, per_b)
    C = CONV_C

    def glu(z):
        return z[:, :C] * jax.nn.sigmoid(z[:, C:])

    ubuf[0:HALO, :] = jnp.where(pos == 0, 0.0, glu(zp_ref[...]))
    ubuf[HALO:HALO + tt, :] = glu(zc_ref[...])
    ubuf[HALO + tt:2 * HALO + tt, :] = jnp.where(pos == n - 1, 0.0, glu(zn_ref[...]))

    base = HALO - CONV_K // 2
    for cb in range(C // LANES):
        cols = slice(cb * LANES, (cb + 1) * LANES)
        w = w_ref[:, cols]
        bias = b_ref[:, cols]
        for rb in range(tt // rows):
            r0 = rb * rows
            acc = jnp.zeros((rows, LANES), F32)
            for k in range(CONV_K):
                acc = acc + ubuf[r0 + base + k:r0 + base + k + rows, cols] * w[k:k + 1, :]
            cbuf[r0:r0 + rows, cols] = acc + bias

    u = cbuf[...]
    mu = jnp.mean(u, axis=-1, keepdims=True)
    d = u - mu
    var = jnp.mean(d * d, axis=-1, keepdims=True)
    y = d * lax.rsqrt(var + EPS) * g_ref[...] + bb_ref[...]
    o_ref[...] = _silu(y).astype(o_ref.dtype)


def _conv_module(z, conv_w, conv_b, ln_g, ln_b, n_a_tok, len_a, len_b, tt=256, rows=64):
    tok = z.shape[0]
    C = CONV_C
    hb = tt // HALO
    nhb = tok // HALO
    w_p = jnp.concatenate([conv_w, jnp.zeros((32 - CONV_K, C), F32)], axis=0)
    kern = functools.partial(_conv_kernel, tt=tt, n_a=n_a_tok // tt, per_a=len_a // tt,
                             per_b=len_b // tt, rows=rows)
    vec = pl.BlockSpec((1, C), lambda i: (0, 0))
    return pl.pallas_call(
        kern,
        out_shape=jax.ShapeDtypeStruct((tok, C), BF16),
        grid=(tok // tt,),
        in_specs=[
            pl.BlockSpec((tt, 2 * C), lambda i: (i, 0)),
            pl.BlockSpec((HALO, 2 * C), lambda i: (jnp.maximum(i * hb - 1, 0), 0)),
            pl.BlockSpec((HALO, 2 * C), lambda i: (jnp.minimum((i + 1) * hb, nhb - 1), 0)),
            pl.BlockSpec((32, C), lambda i: (0, 0)),
            vec, vec, vec,
        ],
        out_specs=pl.BlockSpec((tt, C), lambda i: (i, 0)),
        scratch_shapes=[pltpu.VMEM((tt + 2 * HALO, C), F32), pltpu.VMEM((tt, C), F32)],
        compiler_params=_params("parallel"),
        name="conv_module",
    )(z, z, z, w_p, conv_b.reshape(1, C), ln_g.reshape(1, C), ln_b.reshape(1, C))


def _log_sigmoid(x):
    return -(jnp.maximum(-x, 0.0) + jnp.log(1.0 + jnp.exp(-jnp.abs(x))))


def _ret_kernel(q_ref, k_ref, v_ref, dl_ref, s0_ref, *rest, C, n_a, per_a, per_b, n_chunks,
                backward):
    if backward:
        of_ref, rg_ref, gn_ref, y_ref, sfin_ref, S, dmat, qd, kd = rest
        c = n_chunks - 1 - pl.program_id(0)
    else:
        o_ref, sfin_ref, S, dmat, qd, kd = rest
        c = pl.program_id(0)
    _, pos, n = _seq_pos(c, n_a, per_a, per_b)
    first = (pos == n - 1) if backward else (pos == 0)
    last = (pos == 0) if backward else (pos == n - 1)
    H, dk, dv = RET_H, RET_DK, RET_DV

    @pl.when(pl.program_id(0) == 0)
    def _():
        row = lax.broadcasted_iota(jnp.int32, (C, LANES), 0).astype(F32)
        col = lax.broadcasted_iota(jnp.int32, (C, LANES), 1).astype(F32)
        for h in range(H):
            lg = _log_sigmoid(dl_ref[h])[0:1, :]
            if backward:
                qd[h] = jnp.exp((C - row) * lg)
                kd[h] = jnp.exp(row * lg)
            else:
                qd[h] = jnp.exp((row + 1.0) * lg)
                kd[h] = jnp.exp((C - 1.0 - row) * lg)
            for cb in range(C // LANES):
                diff = row - (col + cb * LANES)
                if backward:
                    diff = -diff
                dm = jnp.where(diff >= 0, jnp.exp(jnp.maximum(diff, 0.0) * lg), 0.0)
                dmat[h, :, cb * LANES:(cb + 1) * LANES] = dm

    @pl.when(first)
    def _():
        S[...] = s0_ref[...]

    nt = (((1,), (1,)), ((), ()))
    tn = (((0,), (0,)), ((), ()))
    for h in range(H):
        q = q_ref[:, h * dk:(h + 1) * dk].astype(BF16)
        kf = k_ref[:, h * dk:(h + 1) * dk] * (dk ** -0.5)
        v = v_ref[:, h * dv:(h + 1) * dv].astype(BF16)
        lg = _log_sigmoid(dl_ref[h])[0:1, :]
        att = lax.dot_general(q, kf.astype(BF16), nt, preferred_element_type=F32) * dmat[h]
        s_old = S[h]
        o = jnp.dot(att.astype(BF16), v, preferred_element_type=F32)
        o = o + qd[h] * jnp.dot(q, s_old.astype(BF16), preferred_element_type=F32)
        kv = lax.dot_general((kf * kd[h]).astype(BF16), v, tn, preferred_element_type=F32)
        S[h] = jnp.exp(C * lg) * s_old + kv
        if backward:
            t = of_ref[:, h * dv:(h + 1) * dv] + o
            mu = jnp.mean(t, axis=-1, keepdims=True)
            d = t - mu
            var = jnp.mean(d * d, axis=-1, keepdims=True)
            yn = d * lax.rsqrt(var + EPS) * gn_ref[:, h * dv:(h + 1) * dv]
            y_ref[:, h * dv:(h + 1) * dv] = (yn * _silu(rg_ref[:, h * dv:(h + 1) * dv])).astype(
                y_ref.dtype)
        else:
            o_ref[:, h * dv:(h + 1) * dv] = o

    @pl.when(last)
    def _():
        sfin_ref[...] = S[...]


def _retention(z, decay_logit, gn_g, s0_all, n_a_tok, len_a, len_b, C=256):
    tok = z.shape[0]
    H, dk, dv = RET_H, RET_DK, RET_DV
    W = H * dk
    n_chunks = tok // C
    n_a, per_a, per_b = n_a_tok // C, len_a // C, len_b // C
    n_seq = s0_all.shape[0]
    dl = jnp.broadcast_to(decay_logit[:, :, None, None], (2, H, 8, LANES)).astype(F32)
    geo = dict(C=C, n_a=n_a, per_a=per_a, per_b=per_b, n_chunks=n_chunks)

    def order(backward):
        return (lambda g: n_chunks - 1 - g) if backward else (lambda g: g)

    def zspec(col0, backward):
        cidx = col0 // W
        f = order(backward)
        return pl.BlockSpec((C, W), lambda g: (f(g), cidx))

    def seq_of(backward):
        f = order(backward)
        return lambda g: _seq_pos(f(g), n_a, per_a, per_b)[0]

    def common_in(d):
        b = d == 1
        s = seq_of(b)
        return [
            zspec(ZC_RQ, b), zspec(ZC_RK, b), zspec(ZC_RV, b),
            pl.BlockSpec((None, H, 8, LANES), lambda g: (d, 0, 0, 0)),
            pl.BlockSpec((None, None, H, dk, dv), lambda g: (s(g), d, 0, 0, 0)),
        ]

    def sfin_spec(d):
        s = seq_of(d == 1)
        return pl.BlockSpec((None, H, dk, dv), lambda g: (s(g), 0, 0, 0))

    scratch = [pltpu.VMEM((H, dk, dv), F32), pltpu.VMEM((H, C, C), F32),
               pltpu.VMEM((H, C, LANES), F32), pltpu.VMEM((H, C, LANES), F32)]
    sfin_shape = jax.ShapeDtypeStruct((n_seq, H, dk, dv), F32)

    o_f, s_f = pl.pallas_call(
        functools.partial(_ret_kernel, backward=False, **geo),
        out_shape=(jax.ShapeDtypeStruct((tok, W), F32), sfin_shape),
        grid=(n_chunks,),
        in_specs=common_in(0),
        out_specs=(pl.BlockSpec((C, W), lambda g: (g, 0)), sfin_spec(0)),
        scratch_shapes=scratch,
        compiler_params=_params("arbitrary"),
        name="retention_fwd",
    )(z, z, z, dl, s0_all)

    rev = order(True)
    y, s_b = pl.pallas_call(
        functools.partial(_ret_kernel, backward=True, **geo),
        out_shape=(jax.ShapeDtypeStruct((tok, W), BF16), sfin_shape),
        grid=(n_chunks,),
        in_specs=common_in(1) + [
            pl.BlockSpec((C, W), lambda g: (rev(g), 0)),
            zspec(ZC_RG, True),
            pl.BlockSpec((1, W), lambda g: (0, 0)),
        ],
        out_specs=(pl.BlockSpec((C, W), lambda g: (rev(g), 0)), sfin_spec(1)),
        scratch_shapes=scratch,
        compiler_params=_params("arbitrary"),
        name="retention_bwd",
    )(z, z, z, dl, s0_all, o_f, z, gn_g.reshape(1, W))
    return y, jnp.stack([s_f, s_b], axis=1)


def _rope(x, cos, sin):
    lane = lax.broadcasted_iota(jnp.int32, x.shape, 1)
    fwd = pltpu.roll(x, LANES - 16, axis=1)
    bwd = pltpu.roll(x, 16, axis=1)
    partner = jnp.where((lane % 32) < 16, fwd, bwd)
    return x * cos + partner * sin


def _mla_prep_kernel(cq_ref, ckv_ref, kr_ref, cos_ref, sin_ref, qg_ref, kg_ref, wq_ref, wk_ref,
                     wv_ref, q_out, k_out, v_out, ckvn_out):
    cos = cos_ref[...]
    sin = sin_ref[...]
    cqn = _rms(cq_ref[...], qg_ref[...]).astype(BF16)
    qa = jnp.dot(cqn, wq_ref[...], preferred_element_type=F32) * (MLA_SCALE * LOG2E)
    ckvn = _rms(ckv_ref[...], kg_ref[...])
    ckvn_out[...] = ckvn
    kb = ckvn.astype(BF16)
    kn = jnp.dot(kb, wk_ref[...], preferred_element_type=F32)
    v_out[...] = jnp.dot(kb, wv_ref[...], preferred_element_type=F32).astype(BF16)
    kr = _rope(kr_ref[...], cos, sin).astype(BF16)
    for h in range(MLA_H):
        c0 = h * QK_PAD
        q_out[:, c0:c0 + LANES] = qa[:, c0:c0 + LANES].astype(BF16)
        q_out[:, c0 + LANES:c0 + QK_PAD] = _rope(qa[:, c0 + LANES:c0 + QK_PAD], cos, sin).astype(
            BF16)
        k_out[:, c0:c0 + LANES] = kn[:, h * MLA_DN:(h + 1) * MLA_DN].astype(BF16)
        k_out[:, c0 + LANES:c0 + QK_PAD] = kr


def _ctx_prep_kernel(ckv_ref, kr_ref, wk_ref, wv_ref, k_out, v_out):
    kb = ckv_ref[...].astype(BF16)
    kn = jnp.dot(kb, wk_ref[...], preferred_element_type=F32)
    v_out[...] = jnp.dot(kb, wv_ref[...], preferred_element_type=F32).astype(BF16)
    kr = kr_ref[...].astype(BF16)
    for h in range(MLA_H):
        c0 = h * QK_PAD
        k_out[:, c0:c0 + LANES] = kn[:, h * MLA_DN:(h + 1) * MLA_DN].astype(BF16)
        k_out[:, c0 + LANES:c0 + QK_PAD] = kr


def _mla_weights(w_uq, w_ukv):
    H, dn, dr, dv = MLA_H, MLA_DN, MLA_DR, MLA_DV
    wq = w_uq.reshape(-1, H, dn + dr)
    wq = jnp.concatenate([wq, jnp.zeros(wq.shape[:2] + (QK_PAD - dn - dr,), wq.dtype)], axis=-1)
    wkv = w_ukv.reshape(-1, H, dn + dv)
    wk = wkv[..., :dn].reshape(-1, H * dn)
    wv = wkv[..., dn:].reshape(-1, H * dv)
    return wq.reshape(-1, H * QK_PAD).astype(BF16), wk.astype(BF16), wv.astype(BF16)


def _mla_prep(z, rope_cos, rope_sin, q_norm, kv_norm, wq, wk, wv, seg, tm=512):
    tok = z.shape[0]
    H = MLA_H
    per = seg // tm
    tab = pl.BlockSpec((None, tm, LANES), lambda i: (jnp.minimum(i // per, 1), i % per, 0))
    full = lambda a: pl.BlockSpec(a.shape, lambda i: (0,) * a.ndim)
    qg = q_norm.reshape(1, -1)
    kg = kv_norm.reshape(1, -1)
    return pl.pallas_call(
        _mla_prep_kernel,
        out_shape=(jax.ShapeDtypeStruct((tok, H * QK_PAD), BF16),
                   jax.ShapeDtypeStruct((tok, H * QK_PAD), BF16),
                   jax.ShapeDtypeStruct((tok, H * MLA_DV), BF16),
                   jax.ShapeDtypeStruct((tok, MLA_KV_LORA), F32)),
        grid=(tok // tm,),
        in_specs=[
            pl.BlockSpec((tm, MLA_Q_LORA), lambda i: (i, ZC_CQ // MLA_Q_LORA)),
            pl.BlockSpec((tm, MLA_KV_LORA), lambda i: (i, ZC_CKV // MLA_KV_LORA)),
            pl.BlockSpec((tm, LANES), lambda i: (i, ZC_KR // LANES)),
            tab, tab, full(qg), full(kg), full(wq), full(wk), full(wv),
        ],
        out_specs=(pl.BlockSpec((tm, H * QK_PAD), lambda i: (i, 0)),
                   pl.BlockSpec((tm, H * QK_PAD), lambda i: (i, 0)),
                   pl.BlockSpec((tm, H * MLA_DV), lambda i: (i, 0)),
                   pl.BlockSpec((tm, MLA_KV_LORA), lambda i: (i, 0))),
        compiler_params=_params("parallel"),
        name="mla_prep",
    )(z, z, z, rope_cos, rope_sin, qg, kg, wq, wk, wv)


def _ctx_prep(ckv_ctx, kpe_ctx, wk, wv, tm=256):
    rows = ckv_ctx.shape[0]
    H = MLA_H
    full = lambda a: pl.BlockSpec(a.shape, lambda i: (0,) * a.ndim)
    return pl.pallas_call(
        _ctx_prep_kernel,
        out_shape=(jax.ShapeDtypeStruct((rows, H * QK_PAD), BF16),
                   jax.ShapeDtypeStruct((rows, H * MLA_DV), BF16)),
        grid=(rows // tm,),
        in_specs=[pl.BlockSpec((tm, MLA_KV_LORA), lambda i: (i, 0)),
                  pl.BlockSpec((tm, LANES), lambda i: (i, 0)), full(wk), full(wv)],
        out_specs=(pl.BlockSpec((tm, H * QK_PAD), lambda i: (i, 0)),
                   pl.BlockSpec((tm, H * MLA_DV), lambda i: (i, 0))),
        compiler_params=_params("parallel"),
        name="ctx_prep",
    )(ckv_ctx, kpe_ctx, wk, wv)


def _attn_kernel(*refs, has_ctx, tk):
    if has_ctx:
        q_ref, k_ref, v_ref, kc_ref, vc_ref, o_ref = refs
    else:
        q_ref, k_ref, v_ref, o_ref = refs
    nt = (((1,), (1,)), ((), ()))
    q = q_ref[...]
    T = k_ref.shape[0]
    tk = min(tk, T)
    chunks = [(kc_ref, vc_ref, 0, kc_ref.shape[0])] if has_ctx else []
    chunks += [(k_ref, v_ref, c0, tk) for c0 in range(0, T, tk)]
    m = den = acc = None
    for kr, vr, c0, n in chunks:
        s = lax.dot_general(q, kr[c0:c0 + n, :], nt, preferred_element_type=F32)
        mc = jnp.max(s, axis=-1, keepdims=True)
        m_new = mc if m is None else jnp.maximum(m, mc)
        p = jnp.exp2(s - m_new)
        ps = jnp.sum(p, axis=-1, keepdims=True)
        pv = jnp.dot(p.astype(BF16), vr[c0:c0 + n, :], preferred_element_type=F32)
        if m is None:
            den, acc = ps, pv
        else:
            alpha = jnp.exp2(m - m_new)
            den = alpha * den + ps
            acc = alpha * acc + pv
        m = m_new
    o_ref[...] = (acc / den).astype(o_ref.dtype)


def _attention(qp, kp, v, tok0, B, T, ctx=None, tq=256, tk=1024):
    H, dv = MLA_H, MLA_DV
    s0 = tok0 // T
    q0 = tok0 // tq
    nq = T // tq
    in_specs = [
        pl.BlockSpec((tq, QK_PAD), lambda b, h, i: (q0 + b * nq + i, h)),
        pl.BlockSpec((T, QK_PAD), lambda b, h, i: (s0 + b, h)),
        pl.BlockSpec((T, dv), lambda b, h, i: (s0 + b, h)),
    ]
    args = [qp, kp, v]
    if ctx is not None:
        past = ctx[0].shape[0] // B
        in_specs += [pl.BlockSpec((past, QK_PAD), lambda b, h, i: (b, h)),
                     pl.BlockSpec((past, dv), lambda b, h, i: (b, h))]
        args += list(ctx)
    return pl.pallas_call(
        functools.partial(_attn_kernel, has_ctx=ctx is not None, tk=tk),
        out_shape=jax.ShapeDtypeStruct((B * T, H * dv), BF16),
        grid=(B, H, nq),
        in_specs=in_specs,
        out_specs=pl.BlockSpec((tq, dv), lambda b, h, i: (b * nq + i, h)),
        compiler_params=_params("parallel", "parallel", "arbitrary"),
        name="attention_ctx" if ctx is not None else "attention",
    )(*args)


def _route(h, wr_ref, br_ref):
    br = br_ref[...]
    logit = [jnp.sum(h * wr_ref[e:e + 1, :], axis=-1, keepdims=True) + br[:, e:e + 1]
             for e in range(N_EXPERTS)]
    neg = jnp.float32(-jnp.inf)

    def top(vals):
        m = functools.reduce(jnp.maximum, vals)
        idx = jnp.full_like(m, float(N_EXPERTS - 1))
        for e in reversed(range(N_EXPERTS - 1)):
            idx = jnp.where(vals[e] == m, float(e), idx)
        return m, idx

    m1, i1 = top(logit)
    m2, i2 = top([jnp.where(i1 == float(e), neg, logit[e]) for e in range(N_EXPERTS)])
    lane = lax.broadcasted_iota(jnp.int32, (h.shape[0], LANES), 1)
    e = jnp.exp(m2 - m1)
    g1 = 1.0 / (1.0 + e)
    g2 = e / (1.0 + e)
    rec = jnp.where(lane == 0, i1, 0.0)
    rec = jnp.where(lane == 1, i2, rec)
    rec = jnp.where(lane == 2, g1, rec)
    return jnp.where(lane == 3, g2, rec)


def _outproj_kernel(yc_ref, yr_ref, ym_ref, w_ref, x_ref, gate_ref, n1_ref, n2_ref, sc_ref,
                    sh_ref, *rest, routed):
    if routed:
        wr_ref, br_ref, x1_ref, h_ref, r_ref = rest
    else:
        x1_ref, h_ref = rest
    c1 = yc_ref.shape[1]
    c2 = c1 + yr_ref.shape[1]
    y = jnp.dot(yc_ref[...], w_ref[0:c1, :], preferred_element_type=F32)
    y = y + jnp.dot(yr_ref[...], w_ref[c1:c2, :], preferred_element_type=F32)
    y = y + jnp.dot(ym_ref[...], w_ref[c2:, :], preferred_element_type=F32)
    x1 = x_ref[...] + gate_ref[...] * _rms(y, n1_ref[...])
    x1_ref[...] = x1
    h = _rms(x1, n2_ref[...]) * (1.0 + sc_ref[...]) + sh_ref[...]
    h_ref[...] = h.astype(h_ref.dtype)
    if routed:
        r_ref[...] = _route(h, wr_ref, br_ref)


def _outproj(yc, yr, ym, w_out, x, gains, mod, seg, router=None, tm=512):
    tok, D = x.shape
    routed = router is not None
    row = lambda a: pl.BlockSpec((tm, a.shape[1]), lambda i: (i, 0))
    full = lambda a: pl.BlockSpec(a.shape, lambda i: (0,) * a.ndim)
    in_specs = [row(yc), row(yr), row(ym), full(w_out), row(x), _mod_spec(2, tm, seg),
                _gain_spec(1), _gain_spec(2), _mod_spec(4, tm, seg), _mod_spec(3, tm, seg)]
    args = [yc, yr, ym, w_out, x, mod, gains, gains, mod, mod]
    out_shape = [jax.ShapeDtypeStruct((tok, D), F32),
                 jax.ShapeDtypeStruct((tok, D), F32 if routed else BF16)]
    out_specs = [pl.BlockSpec((tm, D), lambda i: (i, 0)), pl.BlockSpec((tm, D), lambda i: (i, 0))]
    if routed:
        in_specs += [full(router[0]), full(router[1])]
        args += list(router)
        out_shape.append(jax.ShapeDtypeStruct((tok, LANES), F32))
        out_specs.append(pl.BlockSpec((tm, LANES), lambda i: (i, 0)))
    return pl.pallas_call(
        functools.partial(_outproj_kernel, routed=routed),
        out_shape=tuple(out_shape),
        grid=(tok // tm,),
        in_specs=in_specs,
        out_specs=tuple(out_specs),
        compiler_params=_params("parallel"),
        name="outproj_routed" if routed else "outproj",
    )(*args)


def _ffn_kernel(h_ref, wg_ref, wu_ref, wd_ref, x_ref, gate_ref, n_ref, o_ref, acc):
    f = pl.program_id(1)

    @pl.when(f == 0)
    def _():
        acc[...] = jnp.zeros_like(acc)

    h = h_ref[...]
    g = jnp.dot(h, wg_ref[...], preferred_element_type=F32)
    u = jnp.dot(h, wu_ref[...], preferred_element_type=F32)
    acc[...] += jnp.dot((_silu(g) * u).astype(BF16), wd_ref[...], preferred_element_type=F32)

    @pl.when(f == pl.num_programs(1) - 1)
    def _():
        o_ref[...] = x_ref[...] + gate_ref[...] * _rms(acc[...], n_ref[...])


def _dense_ffn(h, wg, wu, wd, x, gains, mod, seg, tm=512, tf=512):
    tok, D = x.shape
    F = wg.shape[1]
    return pl.pallas_call(
        _ffn_kernel,
        out_shape=jax.ShapeDtypeStruct((tok, D), F32),
        grid=(tok // tm, F // tf),
        in_specs=[
            pl.BlockSpec((tm, D), lambda i, f: (i, 0)),
            pl.BlockSpec((D, tf), lambda i, f: (0, f)),
            pl.BlockSpec((D, tf), lambda i, f: (0, f)),
            pl.BlockSpec((tf, D), lambda i, f: (f, 0)),
            pl.BlockSpec((tm, D), lambda i, f: (i, 0)),
            _mod_spec(5, tm, seg),
            _gain_spec(3),
        ],
        out_specs=pl.BlockSpec((tm, D), lambda i, f: (i, 0)),
        scratch_shapes=[pltpu.VMEM((tm, D), F32)],
        compiler_params=_params("parallel", "arbitrary"),
        name="dense_ffn",
    )(h, wg, wu, wd, x, mod, gains)


def _gather_rows_kernel(idx_ref, src_ref, dst_ref, sem, *, rows):
    base = pl.program_id(0) * rows

    def start(r, carry):
        pltpu.make_async_copy(src_ref.at[pl.ds(idx_ref[base + r], 1), :],
                              dst_ref.at[pl.ds(r, 1), :], sem).start()
        return carry

    def wait(r, carry):
        pltpu.make_async_copy(src_ref.at[pl.ds(0, 1), :], dst_ref.at[pl.ds(r, 1), :], sem).wait()
        return carry

    lax.fori_loop(0, rows, start, 0)
    lax.fori_loop(0, rows, wait, 0)


def _gather_rows(src, idx, rows=512):
    n = idx.shape[0]
    D = src.shape[1]
    return pl.pallas_call(
        functools.partial(_gather_rows_kernel, rows=rows),
        out_shape=jax.ShapeDtypeStruct((n, D), src.dtype),
        grid_spec=pltpu.PrefetchScalarGridSpec(
            num_scalar_prefetch=1,
            grid=(n // rows,),
            in_specs=[pl.BlockSpec(memory_space=pl.ANY)],
            out_specs=pl.BlockSpec((rows, D), lambda i, idx: (i, 0)),
            scratch_shapes=[pltpu.SemaphoreType.DMA],
        ),
        compiler_params=_params("arbitrary"),
        name="moe_gather",
    )(idx, src)


def _moe_ffn_kernel(te_ref, nu_ref, xs_ref, wg_ref, wu_ref, wd_ref, o_ref, xb, acc):
    i = pl.program_id(0)
    f = pl.program_id(1)
    nf = pl.num_programs(1)
    used = i < nu_ref[0]

    @pl.when(used & (f == 0))
    def _():
        xb[...] = xs_ref[...].astype(BF16)
        acc[...] = jnp.zeros_like(acc)

    @pl.when(used)
    def _():
        x = xb[...]
        g = jnp.dot(x, wg_ref[...], preferred_element_type=F32)
        u = jnp.dot(x, wu_ref[...], preferred_element_type=F32)
        acc[...] += jnp.dot((_silu(g) * u).astype(BF16), wd_ref[...], preferred_element_type=F32)

    @pl.when(used & (f == nf - 1))
    def _():
        o_ref[...] = acc[...]

    @pl.when(jnp.logical_not(used) & (f == nf - 1))
    def _():
        o_ref[...] = jnp.zeros_like(o_ref)


def _moe_ffn(xs, tile_expert, n_used, wg, wu, wd, tm, tf=512):
    n_slots, D = xs.shape
    F = wg.shape[2]
    nf = F // tf
    n_tiles = n_slots // tm

    def fi(i, f, nu):
        return jnp.where(i < nu[0], f, nf - 1)

    return pl.pallas_call(
        _moe_ffn_kernel,
        out_shape=jax.ShapeDtypeStruct((n_slots, D), F32),
        grid_spec=pltpu.PrefetchScalarGridSpec(
            num_scalar_prefetch=2,
            grid=(n_tiles, nf),
            in_specs=[
                pl.BlockSpec((tm, D), lambda i, f, te, nu: (jnp.minimum(i, nu[0] - 1), 0)),
                pl.BlockSpec((None, D, tf), lambda i, f, te, nu: (te[i], 0, fi(i, f, nu))),
                pl.BlockSpec((None, D, tf), lambda i, f, te, nu: (te[i], 0, fi(i, f, nu))),
                pl.BlockSpec((None, tf, D), lambda i, f, te, nu: (te[i], fi(i, f, nu), 0)),
            ],
            out_specs=pl.BlockSpec((tm, D), lambda i, f, te, nu: (i, 0)),
            scratch_shapes=[pltpu.VMEM((tm, D), BF16), pltpu.VMEM((tm, D), F32)],
        ),
        compiler_params=_params("arbitrary", "arbitrary"),
        name="moe_ffn",
    )(tile_expert, n_used, xs, wg, wu, wd)


def _combine_kernel(d_ref, ys_ref, gates_ref, x_ref, gate_ref, n_ref, o_ref, b0, b1, sem,
                    *, rows):
    base = pl.program_id(0) * rows

    def copies(r):
        a = 2 * (base + r)
        return (pltpu.make_async_copy(ys_ref.at[pl.ds(d_ref[a], 1), :],
                                      b0.at[pl.ds(r, 1), :], sem.at[0]),
                pltpu.make_async_copy(ys_ref.at[pl.ds(d_ref[a + 1], 1), :],
                                      b1.at[pl.ds(r, 1), :], sem.at[1]))

    def start(r, carry):
        for cp in copies(r):
            cp.start()
        return carry

    def wait(r, carry):
        for cp in copies(r):
            cp.wait()
        return carry

    lax.fori_loop(0, rows, start, 0)
    lax.fori_loop(0, rows, wait, 0)
    gates = gates_ref[...]
    out = gates[:, 2:3] * b0[...] + gates[:, 3:4] * b1[...]
    o_ref[...] = x_ref[...] + gate_ref[...] * _rms(out, n_ref[...])


def _moe_combine(ys, dest, route, x, gains, mod, seg, rows=256):
    tok, D = x.shape
    return pl.pallas_call(
        functools.partial(_combine_kernel, rows=rows),
        out_shape=jax.ShapeDtypeStruct((tok, D), F32),
        grid_spec=pltpu.PrefetchScalarGridSpec(
            num_scalar_prefetch=1,
            grid=(tok // rows,),
            in_specs=[
                pl.BlockSpec(memory_space=pl.ANY),
                pl.BlockSpec((rows, LANES), lambda i, d: (i, 0)),
                pl.BlockSpec((rows, D), lambda i, d: (i, 0)),
                _mod_spec(5, rows, seg),
                _gain_spec(3),
            ],
            out_specs=pl.BlockSpec((rows, D), lambda i, d: (i, 0)),
            scratch_shapes=[pltpu.VMEM((rows, D), F32), pltpu.VMEM((rows, D), F32),
                            pltpu.SemaphoreType.DMA((2,))],
        ),
        compiler_params=_params("arbitrary"),
        name="moe_combine",
    )(dest, ys, route, x, mod, gains)


def _moe_plan(route, tm):
    tok = route.shape[0]
    n_assign = tok * TOP_K
    n_tiles = -(-n_assign // tm) + N_EXPERTS
    n_slots = n_tiles * tm
    expert = route[:, :TOP_K].astype(jnp.int32).reshape(-1)
    token = jnp.repeat(jnp.arange(tok, dtype=jnp.int32), TOP_K)
    onehot = (expert[:, None] == jnp.arange(N_EXPERTS, dtype=jnp.int32)[None, :]).astype(jnp.int32)
    csum = jnp.cumsum(onehot, axis=0)
    counts = csum[-1]
    padded = (counts + tm - 1) // tm * tm
    padded_end = jnp.cumsum(padded)
    padded_start = padded_end - padded
    dest = jnp.sum(onehot * (csum - 1 + padded_start[None, :]), axis=1).astype(jnp.int32)
    slot_token = jnp.zeros((n_slots,), jnp.int32).at[dest].set(token)
    n_used = (padded_end[-1] // tm).astype(jnp.int32)
    tile_start = jnp.minimum(jnp.arange(n_tiles, dtype=jnp.int32), n_used - 1) * tm
    tile_expert = jnp.minimum(
        jnp.sum((tile_start[:, None] >= padded_end[None, :]).astype(jnp.int32), axis=1),
        N_EXPERTS - 1)
    return slot_token, dest, tile_expert, n_used.reshape(1)


def _moe_layer(h, route, x, wg, wu, wd, gains, mod, seg, tm=512):
    slot_token, dest, tile_expert, n_used = _moe_plan(route, tm)
    xs = _gather_rows(h, slot_token)
    ys = _moe_ffn(xs, tile_expert, n_used, wg, wu, wd, tm)
    return _moe_combine(ys, dest, route, x, gains, mod, seg)


def _rope_tables(n_tokens):
    half = ROPE_AXIS // 2
    t = jnp.arange(n_tokens)
    row = (t // GRID_W).astype(F32)
    col = (t % GRID_W).astype(F32)
    inv_freq = jnp.power(ROPE_BASE, -jnp.arange(0, ROPE_AXIS, 2, dtype=F32) / ROPE_AXIS)
    ar = row[:, None] * inv_freq
    ac = col[:, None] * inv_freq
    pad = LANES - MLA_DR
    cos = jnp.concatenate([jnp.cos(ar), jnp.cos(ar), jnp.cos(ac), jnp.cos(ac),
                           jnp.ones((n_tokens, pad), F32)], axis=1)
    sin = jnp.concatenate([-jnp.sin(ar), jnp.sin(ar), -jnp.sin(ac), jnp.sin(ac),
                           jnp.zeros((n_tokens, pad), F32)], axis=1)
    assert cos.shape[1] == LANES and half * 4 == MLA_DR
    ident = (jnp.ones_like(cos), jnp.zeros_like(sin))
    return jnp.stack([ident[0], cos]), jnp.stack([ident[1], sin])


def kernel(x_prompt, x_sample, cache_mla_ckv, cache_mla_kpe, state_ret, c, c_ctx, w_mod, b_mod, norm_gains, w_in, w_out, conv_w, conv_b, conv_ln_g, conv_ln_b, ret_decay_logit, ret_gn_g, mla_q_norm, mla_w_uq, mla_kv_norm, mla_w_ukv, ffn_w_gate, ffn_w_up, ffn_w_down, moe_w_router, moe_b_router, moe_w_gate, moe_w_up, moe_w_down):
    D = D_MODEL
    n_p = BATCH * SEQ
    n_s = DEC_BATCH * DEC_SEQ
    seg = DEC_SEQ
    assert n_p == seg, "context tokens must fill exactly one conditioning segment"
    n_seg = 1 + DEC_BATCH

    x = jnp.concatenate([x_prompt.reshape(n_p, D), x_sample.reshape(n_s, D)], axis=0)
    cond8 = jnp.concatenate([c_ctx[None, :], c, jnp.zeros((8 - n_seg, D), F32)], axis=0)
    mod = _modulation(cond8, w_mod, b_mod).reshape(DEPTH, 8, 6, 1, D)
    gains = norm_gains.reshape(DEPTH, 4, 1, D)
    rope_cos, rope_sin = _rope_tables(DEC_SEQ)
    kpe_pad = jnp.concatenate(
        [cache_mla_kpe, jnp.zeros(cache_mla_kpe.shape[:-1] + (LANES - MLA_DR,), F32)], axis=-1)

    ckv_layers, kpe_layers, ret_layers = [], [], []
    for l in range(DEPTH):
        w_in_p = jnp.concatenate([w_in[l], jnp.zeros((D, ZW - IN_COLS), F32)], axis=1).astype(BF16)
        z = _inproj(x, gains[l], mod[l], w_in_p, seg)

        y_conv = _conv_module(z, conv_w[l], conv_b[l], conv_ln_g[l], conv_ln_b[l], n_p, SEQ,
                              DEC_SEQ)

        s0_all = jnp.concatenate(
            [jnp.zeros((BATCH, 2, RET_H, RET_DK, RET_DV), F32), state_ret[:, l]], axis=0)
        y_ret, s_fin = _retention(z, ret_decay_logit[l], ret_gn_g[l], s0_all, n_p, SEQ, DEC_SEQ)

        wq, wk, wv = _mla_weights(mla_w_uq[l], mla_w_ukv[l])
        qp, kp, v, ckv_n = _mla_prep(z, rope_cos, rope_sin, mla_q_norm[l], mla_kv_norm[l],
                                     wq, wk, wv, seg)
        k_ctx, v_ctx = _ctx_prep(cache_mla_ckv[:, l].reshape(DEC_BATCH * PAST_LEN, MLA_KV_LORA),
                                 kpe_pad[:, l].reshape(DEC_BATCH * PAST_LEN, LANES), wk, wv)
        y_mla = jnp.concatenate([
            _attention(qp, kp, v, 0, BATCH, SEQ),
            _attention(qp, kp, v, n_p, DEC_BATCH, DEC_SEQ, ctx=(k_ctx, v_ctx), tq=1024, tk=512),
        ], axis=0)

        ckv_layers.append(ckv_n[:n_p].reshape(BATCH, SEQ, MLA_KV_LORA))
        kpe_layers.append(z[:n_p, ZC_KR:ZC_KR + MLA_DR].reshape(BATCH, SEQ, MLA_DR))
        ret_layers.append(s_fin[:BATCH])

        w_out_b = w_out[l].astype(BF16)
        i = l // 2
        if l % 2 == 0:
            x1, h = _outproj(y_conv, y_ret, y_mla, w_out_b, x, gains[l], mod[l], seg)
            x = _dense_ffn(h, ffn_w_gate[i].astype(BF16), ffn_w_up[i].astype(BF16),
                           ffn_w_down[i].astype(BF16), x1, gains[l], mod[l], seg)
        else:
            wr = moe_w_router[i].T
            br = jnp.concatenate([moe_b_router[i], jnp.zeros((LANES - N_EXPERTS,), F32)])[None, :]
            x1, h, route = _outproj(y_conv, y_ret, y_mla, w_out_b, x, gains[l], mod[l], seg,
                                    router=(wr, br))
            x = _moe_layer(h, route, x1, moe_w_gate[i].astype(BF16), moe_w_up[i].astype(BF16),
                           moe_w_down[i].astype(BF16), gains[l], mod[l], seg)

    y_prompt = x[:n_p].reshape(BATCH, SEQ, D)
    y_sample = x[n_p:].reshape(DEC_BATCH, DEC_SEQ, D)
    return (y_prompt, y_sample, jnp.stack(ckv_layers, axis=1), jnp.stack(kpe_layers, axis=1),
            jnp.stack(ret_layers, axis=1))
```

```python
import functools

import jax
import jax.numpy as jnp
from jax import lax
from jax.experimental import pallas as pl
from jax.experimental.pallas import tpu as pltpu

F32 = jnp.float32
BF16 = jnp.bfloat16

D_MODEL = 2048
BATCH = 16
SEQ = 256
DEPTH = 2
DEC_BATCH = 4
DEC_SEQ = 4096
PAST_LEN = 256
GRID_W = 64
EPS = 1e-6
CONV_C = 512
CONV_K = 31
RET_H = 4
RET_DK = 128
RET_DV = 128
MLA_H = 8
MLA_DN = 128
MLA_DR = 64
MLA_DV = 128
MLA_Q_LORA = 768
MLA_KV_LORA = 256
MLA_SCALE = (MLA_DN + MLA_DR) ** -0.5
LOG2E = 1.4426950408889634
ROPE_BASE = 10000.0
ROPE_AXIS = MLA_DR // 2
D_FF = 5632
N_EXPERTS = 8
TOP_K = 2
D_EXPERT = 7168

LANES = 128
SUBLANES = 8
HALO = 16
QK_PAD = 256
V_PAD = 256

ZC_CONV = 0
ZC_RQ = 2 * CONV_C
ZC_RK = ZC_RQ + RET_H * RET_DK
ZC_RV = ZC_RK + RET_H * RET_DK
ZC_RG = ZC_RV + RET_H * RET_DV
ZC_CQ = ZC_RG + RET_H * RET_DV
ZC_CKV = ZC_CQ + MLA_Q_LORA
ZC_KR = ZC_CKV + MLA_KV_LORA
IN_COLS = ZC_KR + MLA_DR
ZW = 4608


def _params(*sem):
    return pltpu.CompilerParams(dimension_semantics=sem)


def _silu(x):
    return x * jax.nn.sigmoid(x)


def _rms(x, g):
    return x * lax.rsqrt(jnp.mean(x * x, axis=-1, keepdims=True) + EPS) * g


def _mod_kernel(c_ref, w_ref, b_ref, o_ref):
    c = c_ref[...]
    a = _silu(c).astype(BF16)
    o_ref[...] = jnp.dot(a, w_ref[...].astype(BF16), preferred_element_type=F32) + b_ref[...]


def _modulation(cond8, w_mod, b_mod, tn=1024):
    L, D, N = w_mod.shape
    return pl.pallas_call(
        _mod_kernel,
        out_shape=jax.ShapeDtypeStruct((L, 8, N), F32),
        grid=(L, N // tn),
        in_specs=[
            pl.BlockSpec((8, D), lambda l, j: (0, 0)),
            pl.BlockSpec((None, D, tn), lambda l, j: (l, 0, j)),
            pl.BlockSpec((None, 1, tn), lambda l, j: (l, 0, j)),
        ],
        out_specs=pl.BlockSpec((None, 8, tn), lambda l, j: (l, 0, j)),
        compiler_params=_params("parallel", "arbitrary"),
        name="modulation",
    )(cond8, w_mod, b_mod.reshape(L, 1, N))


def _inproj_kernel(x_ref, g_ref, sc_ref, sh_ref, w_ref, o_ref, h_scr, r_scr, *, rb):
    @pl.when(pl.program_id(1) == 0)
    def _():
        x = x_ref[...]
        r = lax.rsqrt(jnp.mean(x * x, axis=-1, keepdims=True) + EPS)
        r_scr[...] = jnp.broadcast_to(r, r_scr.shape)
        a = g_ref[...] * (1.0 + sc_ref[...])
        b = sh_ref[...]
        for r0 in range(0, x_ref.shape[0], rb):
            rr = r_scr[r0:r0 + rb, :]
            for c0 in range(0, x_ref.shape[1], LANES):
                cols = slice(c0, c0 + LANES)
                y = x_ref[r0:r0 + rb, cols] * rr * a[:, cols] + b[:, cols]
                h_scr[r0:r0 + rb, cols] = y.astype(BF16)

    o_ref[...] = jnp.dot(h_scr[...], w_ref[...], preferred_element_type=F32)


def _mod_spec(chunk, tm, seg):
    return pl.BlockSpec((None, None, 1, D_MODEL), lambda i, *_: (i // (seg // tm), chunk, 0, 0))


def _gain_spec(k):
    return pl.BlockSpec((None, 1, D_MODEL), lambda i, *_: (k, 0, 0))


def _inproj(x, gains, mod, w_in_p, seg, tm=1024, tn=768, rb=64):
    tok, D = x.shape
    zw = w_in_p.shape[1]
    return pl.pallas_call(
        functools.partial(_inproj_kernel, rb=rb),
        out_shape=jax.ShapeDtypeStruct((tok, zw), F32),
        grid=(tok // tm, zw // tn),
        in_specs=[
            pl.BlockSpec((tm, D), lambda i, j: (i, 0)),
            _gain_spec(0),
            _mod_spec(1, tm, seg),
            _mod_spec(0, tm, seg),
            pl.BlockSpec((D, tn), lambda i, j: (0, j)),
        ],
        out_specs=pl.BlockSpec((tm, tn), lambda i, j: (i, j)),
        scratch_shapes=[pltpu.VMEM((tm, D), BF16), pltpu.VMEM((tm, LANES), F32)],
        compiler_params=_params("parallel", "arbitrary"),
        name="inproj",
    )(x, gains, mod, mod, w_in_p)


def _seq_pos(t, n_a, per_a, per_b):
    in_a = t < n_a
    tb = t - n_a
    seq = jnp.where(in_a, t // per_a, n_a // per_a + tb // per_b)
    pos = jnp.where(in_a, t % per_a, tb % per_b)
    n = jnp.where(in_a, per_a, per_b)
    return seq, pos, n


def _conv_kernel(zc_ref, zp_ref, zn_ref, w_ref, b_ref, g_ref, bb_ref, o_ref, ubuf, cbuf, shifted,
                 *, tt, n_a, per_a, per_b, rows):
    _, pos, n = _seq_pos(pl.program_id(0), n_a, per_a, per_b)
    C = CONV_C

    def glu(z):
        return z[:, :C] * jax.nn.sigmoid(z[:, C:])

    ubuf[0:HALO, :] = jnp.where(pos == 0, 0.0, glu(zp_ref[...]))
    ubuf[HALO:HALO + tt, :] = glu(zc_ref[...])
    ubuf[HALO + tt:2 * HALO + tt, :] = jnp.where(pos == n - 1, 0.0, glu(zn_ref[...]))

    span = shifted.shape[1]
    for s in range(1, SUBLANES):
        shifted[s - 1] = ubuf[s:s + span, :]

    base = HALO - CONV_K // 2
    for cb in range(C // LANES):
        cols = slice(cb * LANES, (cb + 1) * LANES)
        w = w_ref[:, cols]
        bias = b_ref[:, cols]
        for rb in range(tt // rows):
            r0 = rb * rows
            acc = jnp.zeros((rows, LANES), F32)
            for k in range(CONV_K):
                s = (base + k) % SUBLANES
                a = r0 + base + k - s
                tap = ubuf[a:a + rows, cols] if s == 0 else shifted[s - 1, a:a + rows, cols]
                acc = acc + tap * w[k:k + 1, :]
            cbuf[r0:r0 + rows, cols] = acc + bias

    u = cbuf[...]
    mu = jnp.mean(u, axis=-1, keepdims=True)
    d = u - mu
    var = jnp.mean(d * d, axis=-1, keepdims=True)
    y = d * lax.rsqrt(var + EPS) * g_ref[...] + bb_ref[...]
    o_ref[...] = _silu(y).astype(o_ref.dtype)


def _conv_module(z, conv_w, conv_b, ln_g, ln_b, n_a_tok, len_a, len_b, tt=256, rows=64):
    tok = z.shape[0]
    C = CONV_C
    hb = tt // HALO
    nhb = tok // HALO
    w_p = jnp.concatenate([conv_w, jnp.zeros((32 - CONV_K, C), F32)], axis=0)
    kern = functools.partial(_conv_kernel, tt=tt, n_a=n_a_tok // tt, per_a=len_a // tt,
                             per_b=len_b // tt, rows=rows)
    vec = pl.BlockSpec((1, C), lambda i: (0, 0))
    return pl.pallas_call(
        kern,
        out_shape=jax.ShapeDtypeStruct((tok, C), BF16),
        grid=(tok // tt,),
        in_specs=[
            pl.BlockSpec((tt, 2 * C), lambda i: (i, 0)),
            pl.BlockSpec((HALO, 2 * C), lambda i: (jnp.maximum(i * hb - 1, 0), 0)),
            pl.BlockSpec((HALO, 2 * C), lambda i: (jnp.minimum((i + 1) * hb, nhb - 1), 0)),
            pl.BlockSpec((32, C), lambda i: (0, 0)),
            vec, vec, vec,
        ],
        out_specs=pl.BlockSpec((tt, C), lambda i: (i, 0)),
        scratch_shapes=[pltpu.VMEM((tt + 2 * HALO, C), F32), pltpu.VMEM((tt, C), F32),
                        pltpu.VMEM((SUBLANES - 1, tt + 2 * HALO - SUBLANES, C), F32)],
        compiler_params=_params("parallel"),
        name="conv_module",
    )(z, z, z, w_p, conv_b.reshape(1, C), ln_g.reshape(1, C), ln_b.reshape(1, C))


def _log_sigmoid(x):
    return -(jnp.maximum(-x, 0.0) + jnp.log(1.0 + jnp.exp(-jnp.abs(x))))


def _ret_kernel(q_ref, k_ref, v_ref, dl_ref, s0_ref, *rest, C, n_a, per_a, per_b, n_chunks,
                backward):
    if backward:
        of_ref, rg_ref, gn_ref, y_ref, sfin_ref, S, dmat, qd, kd = rest
        c = n_chunks - 1 - pl.program_id(0)
    else:
        o_ref, sfin_ref, S, dmat, qd, kd = rest
        c = pl.program_id(0)
    _, pos, n = _seq_pos(c, n_a, per_a, per_b)
    first = (pos == n - 1) if backward else (pos == 0)
    last = (pos == 0) if backward else (pos == n - 1)
    H, dk, dv = RET_H, RET_DK, RET_DV

    @pl.when(pl.program_id(0) == 0)
    def _():
        row = lax.broadcasted_iota(jnp.int32, (C, LANES), 0).astype(F32)
        col = lax.broadcasted_iota(jnp.int32, (C, LANES), 1).astype(F32)
        for h in range(H):
            lg = _log_sigmoid(dl_ref[h])[0:1, :]
            if backward:
                qd[h] = jnp.exp((C - row) * lg)
                kd[h] = jnp.exp(row * lg)
            else:
                qd[h] = jnp.exp((row + 1.0) * lg)
                kd[h] = jnp.exp((C - 1.0 - row) * lg)
            for cb in range(C // LANES):
                diff = row - (col + cb * LANES)
                if backward:
                    diff = -diff
                dm = jnp.where(diff >= 0, jnp.exp(jnp.maximum(diff, 0.0) * lg), 0.0)
                dmat[h, :, cb * LANES:(cb + 1) * LANES] = dm

    @pl.when(first)
    def _():
        S[...] = s0_ref[...]

    nt = (((1,), (1,)), ((), ()))
    tn = (((0,), (0,)), ((), ()))
    for h in range(H):
        q = q_ref[:, h * dk:(h + 1) * dk].astype(BF16)
        kf = k_ref[:, h * dk:(h + 1) * dk] * (dk ** -0.5)
        v = v_ref[:, h * dv:(h + 1) * dv].astype(BF16)
        lg = _log_sigmoid(dl_ref[h])[0:1, :]
        att = lax.dot_general(q, kf.astype(BF16), nt, preferred_element_type=F32) * dmat[h]
        s_old = S[h]
        o = jnp.dot(att.astype(BF16), v, preferred_element_type=F32)
        o = o + qd[h] * jnp.dot(q, s_old.astype(BF16), preferred_element_type=F32)
        kv = lax.dot_general((kf * kd[h]).astype(BF16), v, tn, preferred_element_type=F32)
        S[h] = jnp.exp(C * lg) * s_old + kv
        if backward:
            t = of_ref[:, h * dv:(h + 1) * dv] + o
            mu = jnp.mean(t, axis=-1, keepdims=True)
            d = t - mu
            var = jnp.mean(d * d, axis=-1, keepdims=True)
            yn = d * lax.rsqrt(var + EPS) * gn_ref[:, h * dv:(h + 1) * dv]
            y_ref[:, h * dv:(h + 1) * dv] = (yn * _silu(rg_ref[:, h * dv:(h + 1) * dv])).astype(
                y_ref.dtype)
        else:
            o_ref[:, h * dv:(h + 1) * dv] = o

    @pl.when(last)
    def _():
        sfin_ref[...] = S[...]


def _retention(z, decay_logit, gn_g, s0_all, n_a_tok, len_a, len_b, C=256):
    tok = z.shape[0]
    H, dk, dv = RET_H, RET_DK, RET_DV
    W = H * dk
    n_chunks = tok // C
    n_a, per_a, per_b = n_a_tok // C, len_a // C, len_b // C
    n_seq = s0_all.shape[0]
    dl = jnp.broadcast_to(decay_logit[:, :, None, None], (2, H, 8, LANES)).astype(F32)
    geo = dict(C=C, n_a=n_a, per_a=per_a, per_b=per_b, n_chunks=n_chunks)

    def order(backward):
        return (lambda g: n_chunks - 1 - g) if backward else (lambda g: g)

    def zspec(col0, backward):
        cidx = col0 // W
        f = order(backward)
        return pl.BlockSpec((C, W), lambda g: (f(g), cidx))

    def seq_of(backward):
        f = order(backward)
        return lambda g: _seq_pos(f(g), n_a, per_a, per_b)[0]

    def common_in(d):
        b = d == 1
        s = seq_of(b)
        return [
            zspec(ZC_RQ, b), zspec(ZC_RK, b), zspec(ZC_RV, b),
            pl.BlockSpec((None, H, 8, LANES), lambda g: (d, 0, 0, 0)),
            pl.BlockSpec((None, None, H, dk, dv), lambda g: (s(g), d, 0, 0, 0)),
        ]

    def sfin_spec(d):
        s = seq_of(d == 1)
        return pl.BlockSpec((None, H, dk, dv), lambda g: (s(g), 0, 0, 0))

    scratch = [pltpu.VMEM((H, dk, dv), F32), pltpu.VMEM((H, C, C), F32),
               pltpu.VMEM((H, C, LANES), F32), pltpu.VMEM((H, C, LANES), F32)]
    sfin_shape = jax.ShapeDtypeStruct((n_seq, H, dk, dv), F32)

    o_f, s_f = pl.pallas_call(
        functools.partial(_ret_kernel, backward=False, **geo),
        out_shape=(jax.ShapeDtypeStruct((tok, W), F32), sfin_shape),
        grid=(n_chunks,),
        in_specs=common_in(0),
        out_specs=(pl.BlockSpec((C, W), lambda g: (g, 0)), sfin_spec(0)),
        scratch_shapes=scratch,
        compiler_params=_params("arbitrary"),
        name="retention_fwd",
    )(z, z, z, dl, s0_all)

    rev = order(True)
    y, s_b = pl.pallas_call(
        functools.partial(_ret_kernel, backward=True, **geo),
        out_shape=(jax.ShapeDtypeStruct((tok, W), BF16), sfin_shape),
        grid=(n_chunks,),
        in_specs=common_in(1) + [
            pl.BlockSpec((C, W), lambda g: (rev(g), 0)),
            zspec(ZC_RG, True),
            pl.BlockSpec((1, W), lambda g: (0, 0)),
        ],
        out_specs=(pl.BlockSpec((C, W), lambda g: (rev(g), 0)), sfin_spec(1)),
        scratch_shapes=scratch,
        compiler_params=_params("arbitrary"),
        name="retention_bwd",
    )(z, z, z, dl, s0_all, o_f, z, gn_g.reshape(1, W))
    return y, jnp.stack([s_f, s_b], axis=1)


def _rope(x, cos, sin):
    lane = lax.broadcasted_iota(jnp.int32, x.shape, 1)
    fwd = pltpu.roll(x, LANES - 16, axis=1)
    bwd = pltpu.roll(x, 16, axis=1)
    partner = jnp.where((lane % 32) < 16, fwd, bwd)
    return x * cos + partner * sin


def _mla_prep_kernel(cq_ref, ckv_ref, kr_ref, cos_ref, sin_ref, qg_ref, kg_ref, wq_ref, wk_ref,
                     wv_ref, q_out, k_out, v_out, ckvn_out):
    cos = cos_ref[...]
    sin = sin_ref[...]
    cqn = _rms(cq_ref[...], qg_ref[...]).astype(BF16)
    qa = jnp.dot(cqn, wq_ref[...], preferred_element_type=F32) * (MLA_SCALE * LOG2E)
    ckvn = _rms(ckv_ref[...], kg_ref[...])
    ckvn_out[...] = ckvn
    kb = ckvn.astype(BF16)
    for h in range(MLA_H):
        c0 = h * QK_PAD
        q_out[:, c0:c0 + LANES] = qa[:, c0:c0 + LANES].astype(BF16)
        q_out[:, c0 + LANES:c0 + QK_PAD] = _rope(qa[:, c0 + LANES:c0 + QK_PAD], cos, sin).astype(
            BF16)
    _store_kv(kb, _rope(kr_ref[...], cos, sin), wk_ref, wv_ref, k_out, v_out)


def _store_kv(kb, kr, wk_ref, wv_ref, k_out, v_out):
    kn = jnp.dot(kb, wk_ref[...], preferred_element_type=F32)
    vv = jnp.dot(kb, wv_ref[...], preferred_element_type=F32)
    kr = kr.astype(BF16)
    ones = jnp.ones((kb.shape[0], V_PAD - MLA_DV), BF16)
    for h in range(MLA_H):
        k_out[:, h * QK_PAD:h * QK_PAD + LANES] = kn[:, h * MLA_DN:(h + 1) * MLA_DN].astype(BF16)
        k_out[:, h * QK_PAD + LANES:(h + 1) * QK_PAD] = kr
        v_out[:, h * V_PAD:h * V_PAD + MLA_DV] = vv[:, h * MLA_DV:(h + 1) * MLA_DV].astype(BF16)
        v_out[:, h * V_PAD + MLA_DV:(h + 1) * V_PAD] = ones


def _ctx_prep_kernel(ckv_ref, kr_ref, wk_ref, wv_ref, k_out, v_out):
    _store_kv(ckv_ref[...].astype(BF16), kr_ref[...], wk_ref, wv_ref, k_out, v_out)


def _mla_weights(w_uq, w_ukv):
    H, dn, dr, dv = MLA_H, MLA_DN, MLA_DR, MLA_DV
    wq = w_uq.reshape(-1, H, dn + dr)
    wq = jnp.concatenate([wq, jnp.zeros(wq.shape[:2] + (QK_PAD - dn - dr,), wq.dtype)], axis=-1)
    wkv = w_ukv.reshape(-1, H, dn + dv)
    wk = wkv[..., :dn].reshape(-1, H * dn)
    wv = wkv[..., dn:].reshape(-1, H * dv)
    return wq.reshape(-1, H * QK_PAD).astype(BF16), wk.astype(BF16), wv.astype(BF16)


def _mla_prep(z, rope_cos, rope_sin, q_norm, kv_norm, wq, wk, wv, seg, tm=512):
    tok = z.shape[0]
    H = MLA_H
    per = seg // tm
    tab = pl.BlockSpec((None, tm, LANES), lambda i: (jnp.minimum(i // per, 1), i % per, 0))
    full = lambda a: pl.BlockSpec(a.shape, lambda i: (0,) * a.ndim)
    qg = q_norm.reshape(1, -1)
    kg = kv_norm.reshape(1, -1)
    return pl.pallas_call(
        _mla_prep_kernel,
        out_shape=(jax.ShapeDtypeStruct((tok, H * QK_PAD), BF16),
                   jax.ShapeDtypeStruct((tok, H * QK_PAD), BF16),
                   jax.ShapeDtypeStruct((tok, H * V_PAD), BF16),
                   jax.ShapeDtypeStruct((tok, MLA_KV_LORA), F32)),
        grid=(tok // tm,),
        in_specs=[
            pl.BlockSpec((tm, MLA_Q_LORA), lambda i: (i, ZC_CQ // MLA_Q_LORA)),
            pl.BlockSpec((tm, MLA_KV_LORA), lambda i: (i, ZC_CKV // MLA_KV_LORA)),
            pl.BlockSpec((tm, LANES), lambda i: (i, ZC_KR // LANES)),
            tab, tab, full(qg), full(kg), full(wq), full(wk), full(wv),
        ],
        out_specs=(pl.BlockSpec((tm, H * QK_PAD), lambda i: (i, 0)),
                   pl.BlockSpec((tm, H * QK_PAD), lambda i: (i, 0)),
                   pl.BlockSpec((tm, H * V_PAD), lambda i: (i, 0)),
                   pl.BlockSpec((tm, MLA_KV_LORA), lambda i: (i, 0))),
        compiler_params=_params("parallel"),
        name="mla_prep",
    )(z, z, z, rope_cos, rope_sin, qg, kg, wq, wk, wv)


def _ctx_prep(ckv_ctx, kpe_ctx, wk, wv, tm=256):
    rows = ckv_ctx.shape[0]
    H = MLA_H
    full = lambda a: pl.BlockSpec(a.shape, lambda i: (0,) * a.ndim)
    return pl.pallas_call(
        _ctx_prep_kernel,
        out_shape=(jax.ShapeDtypeStruct((rows, H * QK_PAD), BF16),
                   jax.ShapeDtypeStruct((rows, H * V_PAD), BF16)),
        grid=(rows // tm,),
        in_specs=[pl.BlockSpec((tm, MLA_KV_LORA), lambda i: (i, 0)),
                  pl.BlockSpec((tm, LANES), lambda i: (i, 0)), full(wk), full(wv)],
        out_specs=(pl.BlockSpec((tm, H * QK_PAD), lambda i: (i, 0)),
                   pl.BlockSpec((tm, H * V_PAD), lambda i: (i, 0))),
        compiler_params=_params("parallel"),
        name="ctx_prep",
    )(ckv_ctx, kpe_ctx, wk, wv)


def _attn_kernel(*refs, nh, has_ctx, tk, rb):
    if has_ctx:
        q_ref, k_ref, v_ref, kc_ref, vc_ref, o_ref, *scr = refs
    else:
        q_ref, k_ref, v_ref, o_ref, *scr = refs
    s_bufs, p_bufs, m_bufs, acc = scr[0:2], scr[2:4], scr[4:6], scr[6]
    nt = (((1,), (1,)), ((), ()))
    tq = q_ref.shape[0]
    T = k_ref.shape[0]
    tk = min(tk, T)
    dv = MLA_DV
    buf = 0
    for hh in range(nh):
        qc = slice(hh * QK_PAD, (hh + 1) * QK_PAD)
        vc = slice(hh * V_PAD, (hh + 1) * V_PAD)
        q = q_ref[:, qc]
        chunks = [(kc_ref, vc_ref, 0, kc_ref.shape[0])] if has_ctx else []
        chunks += [(k_ref, v_ref, c0, tk) for c0 in range(0, T, tk)]
        m = None
        for kr, vr, c0, n in chunks:
            sb, pb, mb = s_bufs[buf], p_bufs[buf], m_bufs[buf]
            buf = 1 - buf
            sb[:, 0:n] = lax.dot_general(q, kr[c0:c0 + n, qc], nt, preferred_element_type=F32)
            m_new = jnp.max(sb[:, 0:n], axis=-1, keepdims=True)
            if m is not None:
                m_new = jnp.maximum(m, m_new)
                alpha = jnp.exp2(m - m_new)
            mb[...] = jnp.broadcast_to(m_new, (tq, LANES))
            for r0 in range(0, tq, rb):
                mrow = mb[r0:r0 + rb, :]
                for cb in range(0, n, LANES):
                    p = jnp.exp2(sb[r0:r0 + rb, cb:cb + LANES] - mrow)
                    pb[r0:r0 + rb, cb:cb + LANES] = p.astype(BF16)
            pv = jnp.dot(pb[:, 0:n], vr[c0:c0 + n, vc], preferred_element_type=F32)
            acc[...] = pv if m is None else alpha * acc[...] + pv
            m = m_new
        a = acc[...]
        o_ref[:, hh * dv:(hh + 1) * dv] = (a[:, 0:dv] / a[:, dv:2 * dv]).astype(o_ref.dtype)


def _attention(qp, kp, v, tok0, B, T, ctx=None, nh=1, tq=256, tk=512, rb=64):
    H, dv = MLA_H, MLA_DV
    s0 = tok0 // T
    q0 = tok0 // tq
    nq = T // tq
    in_specs = [
        pl.BlockSpec((tq, nh * QK_PAD), lambda b, h, i: (q0 + b * nq + i, h)),
        pl.BlockSpec((T, nh * QK_PAD), lambda b, h, i: (s0 + b, h)),
        pl.BlockSpec((T, nh * V_PAD), lambda b, h, i: (s0 + b, h)),
    ]
    args = [qp, kp, v]
    width = min(tk, T)
    if ctx is not None:
        past = ctx[0].shape[0] // B
        width = max(width, past)
        in_specs += [pl.BlockSpec((past, nh * QK_PAD), lambda b, h, i: (b, h)),
                     pl.BlockSpec((past, nh * V_PAD), lambda b, h, i: (b, h))]
        args += list(ctx)
    s_buf = pltpu.VMEM((tq, width), F32)
    p_buf = pltpu.VMEM((tq, width), BF16)
    m_buf = pltpu.VMEM((tq, LANES), F32)
    return pl.pallas_call(
        functools.partial(_attn_kernel, nh=nh, has_ctx=ctx is not None, tk=tk, rb=rb),
        out_shape=jax.ShapeDtypeStruct((B * T, H * dv), BF16),
        grid=(B, H // nh, nq),
        in_specs=in_specs,
        out_specs=pl.BlockSpec((tq, nh * dv), lambda b, h, i: (b * nq + i, h)),
        scratch_shapes=[s_buf, s_buf, p_buf, p_buf, m_buf, m_buf, pltpu.VMEM((tq, V_PAD), F32)],
        compiler_params=_params("parallel", "parallel", "arbitrary"),
        name="attention_ctx" if ctx is not None else "attention",
    )(*args)


def _route(h, wr_ref, br_ref):
    br = br_ref[...]
    logit = [jnp.sum(h * wr_ref[e:e + 1, :], axis=-1, keepdims=True) + br[:, e:e + 1]
             for e in range(N_EXPERTS)]
    neg = jnp.float32(-jnp.inf)

    def top(vals):
        m = functools.reduce(jnp.maximum, vals)
        idx = jnp.full_like(m, float(N_EXPERTS - 1))
        for e in reversed(range(N_EXPERTS - 1)):
            idx = jnp.where(vals[e] == m, float(e), idx)
        return m, idx

    m1, i1 = top(logit)
    m2, i2 = top([jnp.where(i1 == float(e), neg, logit[e]) for e in range(N_EXPERTS)])
    lane = lax.broadcasted_iota(jnp.int32, (h.shape[0], LANES), 1)
    e = jnp.exp(m2 - m1)
    g1 = 1.0 / (1.0 + e)
    g2 = e / (1.0 + e)
    rec = jnp.where(lane == 0, i1, 0.0)
    rec = jnp.where(lane == 1, i2, rec)
    rec = jnp.where(lane == 2, g1, rec)
    return jnp.where(lane == 3, g2, rec)


def _outproj_kernel(yc_ref, yr_ref, ym_ref, w_ref, x_ref, gate_ref, n1_ref, n2_ref, sc_ref,
                    sh_ref, *rest, routed):
    if routed:
        wr_ref, br_ref, x1_ref, h_ref, r_ref = rest
    else:
        x1_ref, h_ref = rest
    c1 = yc_ref.shape[1]
    c2 = c1 + yr_ref.shape[1]
    y = jnp.dot(yc_ref[...], w_ref[0:c1, :], preferred_element_type=F32)
    y = y + jnp.dot(yr_ref[...], w_ref[c1:c2, :], preferred_element_type=F32)
    y = y + jnp.dot(ym_ref[...], w_ref[c2:, :], preferred_element_type=F32)
    x1 = x_ref[...] + gate_ref[...] * _rms(y, n1_ref[...])
    x1_ref[...] = x1
    h = _rms(x1, n2_ref[...]) * (1.0 + sc_ref[...]) + sh_ref[...]
    h_ref[...] = h.astype(h_ref.dtype)
    if routed:
        r_ref[...] = _route(h, wr_ref, br_ref)


def _outproj(yc, yr, ym, w_out, x, gains, mod, seg, router=None, tm=512):
    tok, D = x.shape
    routed = router is not None
    row = lambda a: pl.BlockSpec((tm, a.shape[1]), lambda i: (i, 0))
    full = lambda a: pl.BlockSpec(a.shape, lambda i: (0,) * a.ndim)
    in_specs = [row(yc), row(yr), row(ym), full(w_out), row(x), _mod_spec(2, tm, seg),
                _gain_spec(1), _gain_spec(2), _mod_spec(4, tm, seg), _mod_spec(3, tm, seg)]
    args = [yc, yr, ym, w_out, x, mod, gains, gains, mod, mod]
    out_shape = [jax.ShapeDtypeStruct((tok, D), F32),
                 jax.ShapeDtypeStruct((tok, D), F32 if routed else BF16)]
    out_specs = [pl.BlockSpec((tm, D), lambda i: (i, 0)), pl.BlockSpec((tm, D), lambda i: (i, 0))]
    if routed:
        in_specs += [full(router[0]), full(router[1])]
        args += list(router)
        out_shape.append(jax.ShapeDtypeStruct((tok, LANES), F32))
        out_specs.append(pl.BlockSpec((tm, LANES), lambda i: (i, 0)))
    return pl.pallas_call(
        functools.partial(_outproj_kernel, routed=routed),
        out_shape=tuple(out_shape),
        grid=(tok // tm,),
        in_specs=in_specs,
        out_specs=tuple(out_specs),
        compiler_params=_params("parallel"),
        name="outproj_routed" if routed else "outproj",
    )(*args)


def _ffn_kernel(h_ref, wg_ref, wu_ref, wd_ref, x_ref, gate_ref, n_ref, o_ref, acc):
    f = pl.program_id(1)

    @pl.when(f == 0)
    def _():
        acc[...] = jnp.zeros_like(acc)

    h = h_ref[...]
    g = jnp.dot(h, wg_ref[...], preferred_element_type=F32)
    u = jnp.dot(h, wu_ref[...], preferred_element_type=F32)
    acc[...] += jnp.dot((_silu(g) * u).astype(BF16), wd_ref[...], preferred_element_type=F32)

    @pl.when(f == pl.num_programs(1) - 1)
    def _():
        o_ref[...] = x_ref[...] + gate_ref[...] * _rms(acc[...], n_ref[...])


def _dense_ffn(h, wg, wu, wd, x, gains, mod, seg, tm=512, tf=512):
    tok, D = x.shape
    F = wg.shape[1]
    return pl.pallas_call(
        _ffn_kernel,
        out_shape=jax.ShapeDtypeStruct((tok, D), F32),
        grid=(tok // tm, F // tf),
        in_specs=[
            pl.BlockSpec((tm, D), lambda i, f: (i, 0)),
            pl.BlockSpec((D, tf), lambda i, f: (0, f)),
            pl.BlockSpec((D, tf), lambda i, f: (0, f)),
            pl.BlockSpec((tf, D), lambda i, f: (f, 0)),
            pl.BlockSpec((tm, D), lambda i, f: (i, 0)),
            _mod_spec(5, tm, seg),
            _gain_spec(3),
        ],
        out_specs=pl.BlockSpec((tm, D), lambda i, f: (i, 0)),
        scratch_shapes=[pltpu.VMEM((tm, D), F32)],
        compiler_params=_params("parallel", "arbitrary"),
        name="dense_ffn",
    )(h, wg, wu, wd, x, mod, gains)


def _moe_ffn_kernel(te_ref, nu_ref, tok_ref, h_ref, wg_ref, wu_ref, wd_ref, o_ref, stage, xb, acc,
                    sem, *, tm, chunk):
    i = pl.program_id(0)
    f = pl.program_id(1)
    nf = pl.num_programs(1)
    nu = nu_ref[0]
    used = i < nu
    slot = i % 2

    def request(tile, sl, lo, count):
        def body(r, carry):
            pltpu.make_async_copy(h_ref.at[pl.ds(tok_ref[tile * tm + lo + r], 1), :],
                                  stage.at[sl, pl.ds(lo + r, 1), :], sem.at[sl]).start()
            return carry
        lax.fori_loop(0, count, body, 0, unroll=SUBLANES)

    @pl.when((i == 0) & (f == 0))
    def _():
        request(0, 0, 0, tm)

    @pl.when(used & (f == 0))
    def _():
        pltpu.make_async_copy(h_ref.at[pl.ds(0, tm), :], stage.at[slot], sem.at[slot]).wait()
        xb[...] = stage[slot].astype(BF16)
        acc[...] = jnp.zeros_like(acc)

    full_steps = tm // chunk

    @pl.when((i + 1 < nu) & (f < full_steps))
    def _():
        request(i + 1, 1 - slot, f * chunk, chunk)

    if tm % chunk:
        @pl.when((i + 1 < nu) & (f == full_steps))
        def _():
            request(i + 1, 1 - slot, full_steps * chunk, tm % chunk)

    @pl.when(used)
    def _():
        x = xb[...]
        g = jnp.dot(x, wg_ref[...], preferred_element_type=F32)
        u = jnp.dot(x, wu_ref[...], preferred_element_type=F32)
        acc[...] += jnp.dot((_silu(g) * u).astype(BF16), wd_ref[...], preferred_element_type=F32)

    @pl.when(used & (f == nf - 1))
    def _():
        o_ref[...] = acc[...]

    @pl.when(jnp.logical_not(used) & (f == nf - 1))
    def _():
        o_ref[...] = jnp.zeros_like(o_ref)


def _moe_ffn(h, slot_token, tile_expert, n_used, wg, wu, wd, tm, tf=512):
    D = h.shape[1]
    n_slots = slot_token.shape[0]
    F = wg.shape[2]
    nf = F // tf
    n_tiles = n_slots // tm

    def fi(i, f, nu):
        return jnp.where(i < nu[0], f, nf - 1)

    chunk = pl.cdiv(pl.cdiv(tm, nf), SUBLANES) * SUBLANES
    assert pl.cdiv(tm, chunk) <= nf and tm % SUBLANES == 0
    return pl.pallas_call(
        functools.partial(_moe_ffn_kernel, tm=tm, chunk=chunk),
        out_shape=jax.ShapeDtypeStruct((n_slots, D), F32),
        grid_spec=pltpu.PrefetchScalarGridSpec(
            num_scalar_prefetch=3,
            grid=(n_tiles, nf),
            in_specs=[
                pl.BlockSpec(memory_space=pl.ANY),
                pl.BlockSpec((None, D, tf), lambda i, f, te, nu, tk: (te[i], 0, fi(i, f, nu))),
                pl.BlockSpec((None, D, tf), lambda i, f, te, nu, tk: (te[i], 0, fi(i, f, nu))),
                pl.BlockSpec((None, tf, D), lambda i, f, te, nu, tk: (te[i], fi(i, f, nu), 0)),
            ],
            out_specs=pl.BlockSpec((tm, D), lambda i, f, te, nu, tk: (i, 0)),
            scratch_shapes=[pltpu.VMEM((2, tm, D), F32), pltpu.VMEM((tm, D), BF16),
                            pltpu.VMEM((tm, D), F32), pltpu.SemaphoreType.DMA((2,))],
        ),
        compiler_params=_params("arbitrary", "arbitrary"),
        name="moe_ffn",
    )(tile_expert, n_used, slot_token, h, wg, wu, wd)


def _combine_kernel(d_ref, ys_ref, gates_ref, x_ref, gate_ref, n_ref, oa_ref, ob_ref, b0, b1, sem,
                    *, rows, n_a):
    i = pl.program_id(0)
    n = pl.num_programs(0)
    slot = i % 2

    def request(step, sl):
        def body(r, carry):
            a = 2 * (step * rows + r)
            pltpu.make_async_copy(ys_ref.at[pl.ds(d_ref[a], 1), :],
                                  b0.at[sl, pl.ds(r, 1), :], sem.at[0, sl]).start()
            pltpu.make_async_copy(ys_ref.at[pl.ds(d_ref[a + 1], 1), :],
                                  b1.at[sl, pl.ds(r, 1), :], sem.at[1, sl]).start()
            return carry
        lax.fori_loop(0, rows, body, 0, unroll=SUBLANES)

    @pl.when(i == 0)
    def _():
        request(0, 0)

    @pl.when(i + 1 < n)
    def _():
        request(i + 1, 1 - slot)

    pltpu.make_async_copy(ys_ref.at[pl.ds(0, rows), :], b0.at[slot], sem.at[0, slot]).wait()
    pltpu.make_async_copy(ys_ref.at[pl.ds(0, rows), :], b1.at[slot], sem.at[1, slot]).wait()
    gates = gates_ref[...]
    out = gates[:, 2:3] * b0[slot] + gates[:, 3:4] * b1[slot]
    res = x_ref[...] + gate_ref[...] * _rms(out, n_ref[...])

    @pl.when(i < n_a)
    def _():
        oa_ref[...] = res

    @pl.when(i >= n_a)
    def _():
        ob_ref[...] = res


def _moe_combine(ys, dest, route, x, gains, mod, seg, n_a_tok, rows=256):
    tok, D = x.shape
    n_a = n_a_tok // rows
    return pl.pallas_call(
        functools.partial(_combine_kernel, rows=rows, n_a=n_a),
        out_shape=(jax.ShapeDtypeStruct((n_a_tok, D), F32),
                   jax.ShapeDtypeStruct((tok - n_a_tok, D), F32)),
        grid_spec=pltpu.PrefetchScalarGridSpec(
            num_scalar_prefetch=1,
            grid=(tok // rows,),
            in_specs=[
                pl.BlockSpec(memory_space=pl.ANY),
                pl.BlockSpec((rows, LANES), lambda i, d: (i, 0)),
                pl.BlockSpec((rows, D), lambda i, d: (i, 0)),
                _mod_spec(5, rows, seg),
                _gain_spec(3),
            ],
            out_specs=(pl.BlockSpec((rows, D), lambda i, d: (jnp.minimum(i, n_a - 1), 0)),
                       pl.BlockSpec((rows, D), lambda i, d: (jnp.maximum(i - n_a, 0), 0))),
            scratch_shapes=[pltpu.VMEM((2, rows, D), F32), pltpu.VMEM((2, rows, D), F32),
                            pltpu.SemaphoreType.DMA((2, 2))],
        ),
        compiler_params=_params("arbitrary"),
        name="moe_combine",
    )(dest, ys, route, x, mod, gains)


def _moe_plan(route, tm):
    tok = route.shape[0]
    n_assign = tok * TOP_K
    n_tiles = -(-n_assign // tm) + N_EXPERTS
    n_slots = n_tiles * tm
    expert = route[:, :TOP_K].astype(jnp.int32).reshape(-1)
    token = jnp.repeat(jnp.arange(tok, dtype=jnp.int32), TOP_K)
    onehot = (expert[:, None] == jnp.arange(N_EXPERTS, dtype=jnp.int32)[None, :]).astype(jnp.int32)
    csum = jnp.cumsum(onehot, axis=0)
    counts = csum[-1]
    padded = (counts + tm - 1) // tm * tm
    padded_end = jnp.cumsum(padded)
    padded_start = padded_end - padded
    dest = jnp.sum(onehot * (csum - 1 + padded_start[None, :]), axis=1).astype(jnp.int32)
    slot_token = jnp.zeros((n_slots,), jnp.int32).at[dest].set(token)
    n_used = (padded_end[-1] // tm).astype(jnp.int32)
    tile_start = jnp.minimum(jnp.arange(n_tiles, dtype=jnp.int32), n_used - 1) * tm
    tile_expert = jnp.minimum(
        jnp.sum((tile_start[:, None] >= padded_end[None, :]).astype(jnp.int32), axis=1),
        N_EXPERTS - 1)
    return slot_token, dest, tile_expert, n_used.reshape(1)


def _moe_layer(h, route, x, wg, wu, wd, gains, mod, seg, n_a_tok, tm=512):
    slot_token, dest, tile_expert, n_used = _moe_plan(route, tm)
    ys = _moe_ffn(h, slot_token, tile_expert, n_used, wg, wu, wd, tm)
    return _moe_combine(ys, dest, route, x, gains, mod, seg, n_a_tok)


def _rope_tables(n_tokens):
    half = ROPE_AXIS // 2
    t = jnp.arange(n_tokens)
    row = (t // GRID_W).astype(F32)
    col = (t % GRID_W).astype(F32)
    inv_freq = jnp.power(ROPE_BASE, -jnp.arange(0, ROPE_AXIS, 2, dtype=F32) / ROPE_AXIS)
    ar = row[:, None] * inv_freq
    ac = col[:, None] * inv_freq
    pad = LANES - MLA_DR
    cos = jnp.concatenate([jnp.cos(ar), jnp.cos(ar), jnp.cos(ac), jnp.cos(ac),
                           jnp.ones((n_tokens, pad), F32)], axis=1)
    sin = jnp.concatenate([-jnp.sin(ar), jnp.sin(ar), -jnp.sin(ac), jnp.sin(ac),
                           jnp.zeros((n_tokens, pad), F32)], axis=1)
    assert cos.shape[1] == LANES and half * 4 == MLA_DR
    ident = (jnp.ones_like(cos), jnp.zeros_like(sin))
    return jnp.stack([ident[0], cos]), jnp.stack([ident[1], sin])


def kernel(x_prompt, x_sample, cache_mla_ckv, cache_mla_kpe, state_ret, c, c_ctx, w_mod, b_mod, norm_gains, w_in, w_out, conv_w, conv_b, conv_ln_g, conv_ln_b, ret_decay_logit, ret_gn_g, mla_q_norm, mla_w_uq, mla_kv_norm, mla_w_ukv, ffn_w_gate, ffn_w_up, ffn_w_down, moe_w_router, moe_b_router, moe_w_gate, moe_w_up, moe_w_down):
    D = D_MODEL
    n_p = BATCH * SEQ
    n_s = DEC_BATCH * DEC_SEQ
    seg = DEC_SEQ
    assert n_p == seg, "context tokens must fill exactly one conditioning segment"
    n_seg = 1 + DEC_BATCH

    x = jnp.concatenate([x_prompt.reshape(n_p, D), x_sample.reshape(n_s, D)], axis=0)
    cond8 = jnp.concatenate([c_ctx[None, :], c, jnp.zeros((8 - n_seg, D), F32)], axis=0)
    mod = _modulation(cond8, w_mod, b_mod).reshape(DEPTH, 8, 6, 1, D)
    gains = norm_gains.reshape(DEPTH, 4, 1, D)
    rope_cos, rope_sin = _rope_tables(DEC_SEQ)
    kpe_pad = jnp.concatenate(
        [cache_mla_kpe, jnp.zeros(cache_mla_kpe.shape[:-1] + (LANES - MLA_DR,), F32)], axis=-1)

    ckv_layers, kpe_layers, ret_layers = [], [], []
    for l in range(DEPTH):
        w_in_p = jnp.concatenate([w_in[l], jnp.zeros((D, ZW - IN_COLS), F32)], axis=1).astype(BF16)
        z = _inproj(x, gains[l], mod[l], w_in_p, seg)

        y_conv = _conv_module(z, conv_w[l], conv_b[l], conv_ln_g[l], conv_ln_b[l], n_p, SEQ,
                              DEC_SEQ)

        s0_all = jnp.concatenate(
            [jnp.zeros((BATCH, 2, RET_H, RET_DK, RET_DV), F32), state_ret[:, l]], axis=0)
        y_ret, s_fin = _retention(z, ret_decay_logit[l], ret_gn_g[l], s0_all, n_p, SEQ, DEC_SEQ)

        wq, wk, wv = _mla_weights(mla_w_uq[l], mla_w_ukv[l])
        qp, kp, v, ckv_n = _mla_prep(z, rope_cos, rope_sin, mla_q_norm[l], mla_kv_norm[l],
                                     wq, wk, wv, seg)
        k_ctx, v_ctx = _ctx_prep(cache_mla_ckv[:, l].reshape(DEC_BATCH * PAST_LEN, MLA_KV_LORA),
                                 kpe_pad[:, l].reshape(DEC_BATCH * PAST_LEN, LANES), wk, wv)
        y_mla = jnp.concatenate([
            _attention(qp, kp, v, 0, BATCH, SEQ, nh=MLA_H, tq=SEQ),
            _attention(qp, kp, v, n_p, DEC_BATCH, DEC_SEQ, ctx=(k_ctx, v_ctx), tq=1024, tk=1024),
        ], axis=0)

        ckv_layers.append(ckv_n[:n_p].reshape(BATCH, SEQ, MLA_KV_LORA))
        kpe_layers.append(z[:n_p, ZC_KR:ZC_KR + MLA_DR].reshape(BATCH, SEQ, MLA_DR))
        ret_layers.append(s_fin[:BATCH])

        w_out_b = w_out[l].astype(BF16)
        i = l // 2
        if l % 2 == 0:
            x1, h = _outproj(y_conv, y_ret, y_mla, w_out_b, x, gains[l], mod[l], seg)
            x = _dense_ffn(h, ffn_w_gate[i].astype(BF16), ffn_w_up[i].astype(BF16),
                           ffn_w_down[i].astype(BF16), x1, gains[l], mod[l], seg)
        else:
            wr = moe_w_router[i].T
            br = jnp.concatenate([moe_b_router[i], jnp.zeros((LANES - N_EXPERTS,), F32)])[None, :]
            x1, h, route = _outproj(y_conv, y_ret, y_mla, w_out_b, x, gains[l], mod[l], seg,
                                    router=(wr, br))
            parts = _moe_layer(h, route, x1, moe_w_gate[i].astype(BF16), moe_w_up[i].astype(BF16),
                               moe_w_down[i].astype(BF16), gains[l], mod[l], seg, n_p)
            x = jnp.concatenate(parts, axis=0) if l + 1 < DEPTH else None
        if l + 1 == DEPTH and l % 2 == 0:
            parts = (x[:n_p], x[n_p:])

    y_prompt = parts[0].reshape(BATCH, SEQ, D)
    y_sample = parts[1].reshape(DEC_BATCH, DEC_SEQ, D)
    return (y_prompt, y_sample, jnp.stack(ckv_layers, axis=1), jnp.stack(kpe_layers, axis=1),
            jnp.stack(ret_layers, axis=1))
```

```python
import functools

import jax
import jax.numpy as jnp
from jax import lax
from jax.experimental import pallas as pl
from jax.experimental.pallas import tpu as pltpu

F32 = jnp.float32
BF16 = jnp.bfloat16

D_MODEL = 2048
BATCH = 16
SEQ = 256
DEPTH = 2
DEC_BATCH = 4
DEC_SEQ = 4096
PAST_LEN = 256
GRID_W = 64
EPS = 1e-6
CONV_C = 512
CONV_K = 31
RET_H = 4
RET_DK = 128
RET_DV = 128
MLA_H = 8
MLA_DN = 128
MLA_DR = 64
MLA_DV = 128
MLA_Q_LORA = 768
MLA_KV_LORA = 256
MLA_SCALE = (MLA_DN + MLA_DR) ** -0.5
LOG2E = 1.4426950408889634
ROPE_BASE = 10000.0
ROPE_AXIS = MLA_DR // 2
D_FF = 5632
N_EXPERTS = 8
TOP_K = 2
D_EXPERT = 7168

LANES = 128
SUBLANES = 8
HALO = 16
QK_PAD = 256
V_PAD = 256

ZC_CONV = 0
ZC_RQ = 2 * CONV_C
ZC_RK = ZC_RQ + RET_H * RET_DK
ZC_RV = ZC_RK + RET_H * RET_DK
ZC_RG = ZC_RV + RET_H * RET_DV
ZC_CQ = ZC_RG + RET_H * RET_DV
ZC_CKV = ZC_CQ + MLA_Q_LORA
ZC_KR = ZC_CKV + MLA_KV_LORA
IN_COLS = ZC_KR + MLA_DR
ZW = 4608


def _params(*sem):
    return pltpu.CompilerParams(dimension_semantics=sem)


def _silu(x):
    return x * jax.nn.sigmoid(x)


def _rms(x, g):
    return x * lax.rsqrt(jnp.mean(x * x, axis=-1, keepdims=True) + EPS) * g


def _mod_kernel(c_ref, w_ref, b_ref, o_ref):
    c = c_ref[...]
    a = _silu(c).astype(BF16)
    o_ref[...] = jnp.dot(a, w_ref[...].astype(BF16), preferred_element_type=F32) + b_ref[...]


def _modulation(cond8, w_mod, b_mod, tn=1024):
    L, D, N = w_mod.shape
    return pl.pallas_call(
        _mod_kernel,
        out_shape=jax.ShapeDtypeStruct((L, 8, N), F32),
        grid=(L, N // tn),
        in_specs=[
            pl.BlockSpec((8, D), lambda l, j: (0, 0)),
            pl.BlockSpec((None, D, tn), lambda l, j: (l, 0, j)),
            pl.BlockSpec((None, 1, tn), lambda l, j: (l, 0, j)),
        ],
        out_specs=pl.BlockSpec((None, 8, tn), lambda l, j: (l, 0, j)),
        compiler_params=_params("parallel", "arbitrary"),
        name="modulation",
    )(cond8, w_mod, b_mod.reshape(L, 1, N))


def _inproj_kernel(x_ref, g_ref, sc_ref, sh_ref, w_ref, o_ref, h_scr, r_scr, *, rb):
    @pl.when(pl.program_id(1) == 0)
    def _():
        x = x_ref[...]
        r = lax.rsqrt(jnp.mean(x * x, axis=-1, keepdims=True) + EPS)
        r_scr[...] = jnp.broadcast_to(r, r_scr.shape)
        a = g_ref[...] * (1.0 + sc_ref[...])
        b = sh_ref[...]
        for r0 in range(0, x_ref.shape[0], rb):
            rr = r_scr[r0:r0 + rb, :]
            for c0 in range(0, x_ref.shape[1], LANES):
                cols = slice(c0, c0 + LANES)
                y = x_ref[r0:r0 + rb, cols] * rr * a[:, cols] + b[:, cols]
                h_scr[r0:r0 + rb, cols] = y.astype(BF16)

    o_ref[...] = jnp.dot(h_scr[...], w_ref[...], preferred_element_type=F32)


def _mod_spec(chunk, tm, seg):
    return pl.BlockSpec((None, None, 1, D_MODEL), lambda i, *_: (i // (seg // tm), chunk, 0, 0))


def _gain_spec(k):
    return pl.BlockSpec((None, 1, D_MODEL), lambda i, *_: (k, 0, 0))


def _inproj(x, gains, mod, w_in_p, seg, tm=1024, tn=768, rb=64):
    tok, D = x.shape
    zw = w_in_p.shape[1]
    return pl.pallas_call(
        functools.partial(_inproj_kernel, rb=rb),
        out_shape=jax.ShapeDtypeStruct((tok, zw), F32),
        grid=(tok // tm, zw // tn),
        in_specs=[
            pl.BlockSpec((tm, D), lambda i, j: (i, 0)),
            _gain_spec(0),
            _mod_spec(1, tm, seg),
            _mod_spec(0, tm, seg),
            pl.BlockSpec((D, tn), lambda i, j: (0, j)),
        ],
        out_specs=pl.BlockSpec((tm, tn), lambda i, j: (i, j)),
        scratch_shapes=[pltpu.VMEM((tm, D), BF16), pltpu.VMEM((tm, LANES), F32)],
        compiler_params=_params("parallel", "arbitrary"),
        name="inproj",
    )(x, gains, mod, mod, w_in_p)


def _seq_pos(t, n_a, per_a, per_b):
    in_a = t < n_a
    tb = t - n_a
    seq = jnp.where(in_a, t // per_a, n_a // per_a + tb // per_b)
    pos = jnp.where(in_a, t % per_a, tb % per_b)
    n = jnp.where(in_a, per_a, per_b)
    return seq, pos, n


def _conv_kernel(zc_ref, zp_ref, zn_ref, w_ref, b_ref, g_ref, bb_ref, o_ref, ubuf, cbuf, shifted,
                 *, tt, n_a, per_a, per_b, rows):
    _, pos, n = _seq_pos(pl.program_id(0), n_a, per_a, per_b)
    C = CONV_C

    def glu(z):
        return z[:, :C] * jax.nn.sigmoid(z[:, C:])

    ubuf[0:HALO, :] = jnp.where(pos == 0, 0.0, glu(zp_ref[...]))
    ubuf[HALO:HALO + tt, :] = glu(zc_ref[...])
    ubuf[HALO + tt:2 * HALO + tt, :] = jnp.where(pos == n - 1, 0.0, glu(zn_ref[...]))

    span = shifted.shape[1]
    for s in range(1, SUBLANES):
        shifted[s - 1] = ubuf[s:s + span, :]

    base = HALO - CONV_K // 2
    for cb in range(C // LANES):
        cols = slice(cb * LANES, (cb + 1) * LANES)
        w = w_ref[:, cols]
        bias = b_ref[:, cols]
        for rb in range(tt // rows):
            r0 = rb * rows
            acc = jnp.zeros((rows, LANES), F32)
            for k in range(CONV_K):
                s = (base + k) % SUBLANES
                a = r0 + base + k - s
                tap = ubuf[a:a + rows, cols] if s == 0 else shifted[s - 1, a:a + rows, cols]
                acc = acc + tap * w[k:k + 1, :]
            cbuf[r0:r0 + rows, cols] = acc + bias

    u = cbuf[...]
    mu = jnp.mean(u, axis=-1, keepdims=True)
    d = u - mu
    var = jnp.mean(d * d, axis=-1, keepdims=True)
    y = d * lax.rsqrt(var + EPS) * g_ref[...] + bb_ref[...]
    o_ref[...] = _silu(y).astype(o_ref.dtype)


def _conv_module(z, conv_w, conv_b, ln_g, ln_b, n_a_tok, len_a, len_b, tt=256, rows=64):
    tok = z.shape[0]
    C = CONV_C
    hb = tt // HALO
    nhb = tok // HALO
    w_p = jnp.concatenate([conv_w, jnp.zeros((32 - CONV_K, C), F32)], axis=0)
    kern = functools.partial(_conv_kernel, tt=tt, n_a=n_a_tok // tt, per_a=len_a // tt,
                             per_b=len_b // tt, rows=rows)
    vec = pl.BlockSpec((1, C), lambda i: (0, 0))
    return pl.pallas_call(
        kern,
        out_shape=jax.ShapeDtypeStruct((tok, C), BF16),
        grid=(tok // tt,),
        in_specs=[
            pl.BlockSpec((tt, 2 * C), lambda i: (i, 0)),
            pl.BlockSpec((HALO, 2 * C), lambda i: (jnp.maximum(i * hb - 1, 0), 0)),
            pl.BlockSpec((HALO, 2 * C), lambda i: (jnp.minimum((i + 1) * hb, nhb - 1), 0)),
            pl.BlockSpec((32, C), lambda i: (0, 0)),
            vec, vec, vec,
        ],
        out_specs=pl.BlockSpec((tt, C), lambda i: (i, 0)),
        scratch_shapes=[pltpu.VMEM((tt + 2 * HALO, C), F32), pltpu.VMEM((tt, C), F32),
                        pltpu.VMEM((SUBLANES - 1, tt + 2 * HALO - SUBLANES, C), F32)],
        compiler_params=_params("parallel"),
        name="conv_module",
    )(z, z, z, w_p, conv_b.reshape(1, C), ln_g.reshape(1, C), ln_b.reshape(1, C))


def _log_sigmoid(x):
    return -(jnp.maximum(-x, 0.0) + jnp.log(1.0 + jnp.exp(-jnp.abs(x))))


def _ret_kernel(q_ref, k_ref, v_ref, dl_ref, s0_ref, *rest, C, n_a, per_a, per_b, n_chunks,
                backward):
    if backward:
        of_ref, rg_ref, gn_ref, y_ref, sfin_ref, S, dmat, qd, kd = rest
        c = n_chunks - 1 - pl.program_id(0)
    else:
        o_ref, sfin_ref, S, dmat, qd, kd = rest
        c = pl.program_id(0)
    _, pos, n = _seq_pos(c, n_a, per_a, per_b)
    first = (pos == n - 1) if backward else (pos == 0)
    last = (pos == 0) if backward else (pos == n - 1)
    H, dk, dv = RET_H, RET_DK, RET_DV

    @pl.when(pl.program_id(0) == 0)
    def _():
        row = lax.broadcasted_iota(jnp.int32, (C, LANES), 0).astype(F32)
        col = lax.broadcasted_iota(jnp.int32, (C, LANES), 1).astype(F32)
        for h in range(H):
            lg = _log_sigmoid(dl_ref[h])[0:1, :]
            if backward:
                qd[h] = jnp.exp((C - row) * lg)
                kd[h] = jnp.exp(row * lg)
            else:
                qd[h] = jnp.exp((row + 1.0) * lg)
                kd[h] = jnp.exp((C - 1.0 - row) * lg)
            for cb in range(C // LANES):
                diff = row - (col + cb * LANES)
                if backward:
                    diff = -diff
                dm = jnp.where(diff >= 0, jnp.exp(jnp.maximum(diff, 0.0) * lg), 0.0)
                dmat[h, :, cb * LANES:(cb + 1) * LANES] = dm

    @pl.when(first)
    def _():
        S[...] = s0_ref[...]

    nt = (((1,), (1,)), ((), ()))
    tn = (((0,), (0,)), ((), ()))
    for h in range(H):
        q = q_ref[:, h * dk:(h + 1) * dk].astype(BF16)
        kf = k_ref[:, h * dk:(h + 1) * dk] * (dk ** -0.5)
        v = v_ref[:, h * dv:(h + 1) * dv].astype(BF16)
        lg = _log_sigmoid(dl_ref[h])[0:1, :]
        att = lax.dot_general(q, kf.astype(BF16), nt, preferred_element_type=F32) * dmat[h]
        s_old = S[h]
        o = jnp.dot(att.astype(BF16), v, preferred_element_type=F32)
        o = o + qd[h] * jnp.dot(q, s_old.astype(BF16), preferred_element_type=F32)
        kv = lax.dot_general((kf * kd[h]).astype(BF16), v, tn, preferred_element_type=F32)
        S[h] = jnp.exp(C * lg) * s_old + kv
        if backward:
            t = of_ref[:, h * dv:(h + 1) * dv] + o
            mu = jnp.mean(t, axis=-1, keepdims=True)
            d = t - mu
            var = jnp.mean(d * d, axis=-1, keepdims=True)
            yn = d * lax.rsqrt(var + EPS) * gn_ref[:, h * dv:(h + 1) * dv]
            y_ref[:, h * dv:(h + 1) * dv] = (yn * _silu(rg_ref[:, h * dv:(h + 1) * dv])).astype(
                y_ref.dtype)
        else:
            o_ref[:, h * dv:(h + 1) * dv] = o

    @pl.when(last)
    def _():
        sfin_ref[...] = S[...]


def _retention(z, decay_logit, gn_g, s0_all, n_a_tok, len_a, len_b, C=256):
    tok = z.shape[0]
    H, dk, dv = RET_H, RET_DK, RET_DV
    W = H * dk
    n_chunks = tok // C
    n_a, per_a, per_b = n_a_tok // C, len_a // C, len_b // C
    n_seq = s0_all.shape[0]
    dl = jnp.broadcast_to(decay_logit[:, :, None, None], (2, H, 8, LANES)).astype(F32)
    geo = dict(C=C, n_a=n_a, per_a=per_a, per_b=per_b, n_chunks=n_chunks)

    def order(backward):
        return (lambda g: n_chunks - 1 - g) if backward else (lambda g: g)

    def zspec(col0, backward):
        cidx = col0 // W
        f = order(backward)
        return pl.BlockSpec((C, W), lambda g: (f(g), cidx))

    def seq_of(backward):
        f = order(backward)
        return lambda g: _seq_pos(f(g), n_a, per_a, per_b)[0]

    def common_in(d):
        b = d == 1
        s = seq_of(b)
        return [
            zspec(ZC_RQ, b), zspec(ZC_RK, b), zspec(ZC_RV, b),
            pl.BlockSpec((None, H, 8, LANES), lambda g: (d, 0, 0, 0)),
            pl.BlockSpec((None, None, H, dk, dv), lambda g: (s(g), d, 0, 0, 0)),
        ]

    def sfin_spec(d):
        s = seq_of(d == 1)
        return pl.BlockSpec((None, H, dk, dv), lambda g: (s(g), 0, 0, 0))

    scratch = [pltpu.VMEM((H, dk, dv), F32), pltpu.VMEM((H, C, C), F32),
               pltpu.VMEM((H, C, LANES), F32), pltpu.VMEM((H, C, LANES), F32)]
    sfin_shape = jax.ShapeDtypeStruct((n_seq, H, dk, dv), F32)

    o_f, s_f = pl.pallas_call(
        functools.partial(_ret_kernel, backward=False, **geo),
        out_shape=(jax.ShapeDtypeStruct((tok, W), F32), sfin_shape),
        grid=(n_chunks,),
        in_specs=common_in(0),
        out_specs=(pl.BlockSpec((C, W), lambda g: (g, 0)), sfin_spec(0)),
        scratch_shapes=scratch,
        compiler_params=_params("arbitrary"),
        name="retention_fwd",
    )(z, z, z, dl, s0_all)

    rev = order(True)
    y, s_b = pl.pallas_call(
        functools.partial(_ret_kernel, backward=True, **geo),
        out_shape=(jax.ShapeDtypeStruct((tok, W), BF16), sfin_shape),
        grid=(n_chunks,),
        in_specs=common_in(1) + [
            pl.BlockSpec((C, W), lambda g: (rev(g), 0)),
            zspec(ZC_RG, True),
            pl.BlockSpec((1, W), lambda g: (0, 0)),
        ],
        out_specs=(pl.BlockSpec((C, W), lambda g: (rev(g), 0)), sfin_spec(1)),
        scratch_shapes=scratch,
        compiler_params=_params("arbitrary"),
        name="retention_bwd",
    )(z, z, z, dl, s0_all, o_f, z, gn_g.reshape(1, W))
    return y, jnp.stack([s_f, s_b], axis=1)


def _rope(x, cos, sin):
    lane = lax.broadcasted_iota(jnp.int32, x.shape, 1)
    fwd = pltpu.roll(x, LANES - 16, axis=1)
    bwd = pltpu.roll(x, 16, axis=1)
    partner = jnp.where((lane % 32) < 16, fwd, bwd)
    return x * cos + partner * sin


def _mla_prep_kernel(cq_ref, ckv_ref, kr_ref, cos_ref, sin_ref, qg_ref, kg_ref, wq_ref, wk_ref,
                     wv_ref, q_out, k_out, v_out, ckvn_out):
    cos = cos_ref[...]
    sin = sin_ref[...]
    cqn = _rms(cq_ref[...], qg_ref[...]).astype(BF16)
    qa = jnp.dot(cqn, wq_ref[...], preferred_element_type=F32) * (MLA_SCALE * LOG2E)
    ckvn = _rms(ckv_ref[...], kg_ref[...])
    ckvn_out[...] = ckvn
    kb = ckvn.astype(BF16)
    for h in range(MLA_H):
        c0 = h * QK_PAD
        q_out[:, c0:c0 + LANES] = qa[:, c0:c0 + LANES].astype(BF16)
        q_out[:, c0 + LANES:c0 + QK_PAD] = _rope(qa[:, c0 + LANES:c0 + QK_PAD], cos, sin).astype(
            BF16)
    _store_kv(kb, _rope(kr_ref[...], cos, sin), wk_ref, wv_ref, k_out, v_out)


def _store_kv(kb, kr, wk_ref, wv_ref, k_out, v_out):
    kn = jnp.dot(kb, wk_ref[...], preferred_element_type=F32)
    vv = jnp.dot(kb, wv_ref[...], preferred_element_type=F32)
    kr = kr.astype(BF16)
    ones = jnp.ones((kb.shape[0], V_PAD - MLA_DV), BF16)
    for h in range(MLA_H):
        k_out[:, h * QK_PAD:h * QK_PAD + LANES] = kn[:, h * MLA_DN:(h + 1) * MLA_DN].astype(BF16)
        k_out[:, h * QK_PAD + LANES:(h + 1) * QK_PAD] = kr
        v_out[:, h * V_PAD:h * V_PAD + MLA_DV] = vv[:, h * MLA_DV:(h + 1) * MLA_DV].astype(BF16)
        v_out[:, h * V_PAD + MLA_DV:(h + 1) * V_PAD] = ones


def _ctx_prep_kernel(ckv_ref, kr_ref, wk_ref, wv_ref, k_out, v_out):
    _store_kv(ckv_ref[...].astype(BF16), kr_ref[...], wk_ref, wv_ref, k_out, v_out)


def _mla_weights(w_uq, w_ukv):
    H, dn, dr, dv = MLA_H, MLA_DN, MLA_DR, MLA_DV
    wq = w_uq.reshape(-1, H, dn + dr)
    wq = jnp.concatenate([wq, jnp.zeros(wq.shape[:2] + (QK_PAD - dn - dr,), wq.dtype)], axis=-1)
    wkv = w_ukv.reshape(-1, H, dn + dv)
    wk = wkv[..., :dn].reshape(-1, H * dn)
    wv = wkv[..., dn:].reshape(-1, H * dv)
    return wq.reshape(-1, H * QK_PAD).astype(BF16), wk.astype(BF16), wv.astype(BF16)


def _mla_prep(z, rope_cos, rope_sin, q_norm, kv_norm, wq, wk, wv, seg, tm=512):
    tok = z.shape[0]
    H = MLA_H
    per = seg // tm
    tab = pl.BlockSpec((None, tm, LANES), lambda i: (jnp.minimum(i // per, 1), i % per, 0))
    full = lambda a: pl.BlockSpec(a.shape, lambda i: (0,) * a.ndim)
    qg = q_norm.reshape(1, -1)
    kg = kv_norm.reshape(1, -1)
    return pl.pallas_call(
        _mla_prep_kernel,
        out_shape=(jax.ShapeDtypeStruct((tok, H * QK_PAD), BF16),
                   jax.ShapeDtypeStruct((tok, H * QK_PAD), BF16),
                   jax.ShapeDtypeStruct((tok, H * V_PAD), BF16),
                   jax.ShapeDtypeStruct((tok, MLA_KV_LORA), F32)),
        grid=(tok // tm,),
        in_specs=[
            pl.BlockSpec((tm, MLA_Q_LORA), lambda i: (i, ZC_CQ // MLA_Q_LORA)),
            pl.BlockSpec((tm, MLA_KV_LORA), lambda i: (i, ZC_CKV // MLA_KV_LORA)),
            pl.BlockSpec((tm, LANES), lambda i: (i, ZC_KR // LANES)),
            tab, tab, full(qg), full(kg), full(wq), full(wk), full(wv),
        ],
        out_specs=(pl.BlockSpec((tm, H * QK_PAD), lambda i: (i, 0)),
                   pl.BlockSpec((tm, H * QK_PAD), lambda i: (i, 0)),
                   pl.BlockSpec((tm, H * V_PAD), lambda i: (i, 0)),
                   pl.BlockSpec((tm, MLA_KV_LORA), lambda i: (i, 0))),
        compiler_params=_params("parallel"),
        name="mla_prep",
    )(z, z, z, rope_cos, rope_sin, qg, kg, wq, wk, wv)


def _ctx_prep(ckv_ctx, kpe_ctx, wk, wv, tm=256):
    rows = ckv_ctx.shape[0]
    H = MLA_H
    full = lambda a: pl.BlockSpec(a.shape, lambda i: (0,) * a.ndim)
    return pl.pallas_call(
        _ctx_prep_kernel,
        out_shape=(jax.ShapeDtypeStruct((rows, H * QK_PAD), BF16),
                   jax.ShapeDtypeStruct((rows, H * V_PAD), BF16)),
        grid=(rows // tm,),
        in_specs=[pl.BlockSpec((tm, MLA_KV_LORA), lambda i: (i, 0)),
                  pl.BlockSpec((tm, LANES), lambda i: (i, 0)), full(wk), full(wv)],
        out_specs=(pl.BlockSpec((tm, H * QK_PAD), lambda i: (i, 0)),
                   pl.BlockSpec((tm, H * V_PAD), lambda i: (i, 0))),
        compiler_params=_params("parallel"),
        name="ctx_prep",
    )(ckv_ctx, kpe_ctx, wk, wv)


def _attn_kernel(*refs, nh, has_ctx, tk, rb):
    if has_ctx:
        q_ref, k_ref, v_ref, kc_ref, vc_ref, o_ref, *scr = refs
    else:
        q_ref, k_ref, v_ref, o_ref, *scr = refs
    s_bufs, p_bufs, m_bufs, acc = scr[0:2], scr[2:4], scr[4:6], scr[6]
    nt = (((1,), (1,)), ((), ()))
    tq = q_ref.shape[0]
    T = k_ref.shape[0]
    tk = min(tk, T)
    dv = MLA_DV
    buf = 0
    for hh in range(nh):
        qc = slice(hh * QK_PAD, (hh + 1) * QK_PAD)
        vc = slice(hh * V_PAD, (hh + 1) * V_PAD)
        q = q_ref[:, qc]
        chunks = [(kc_ref, vc_ref, 0, kc_ref.shape[0])] if has_ctx else []
        chunks += [(k_ref, v_ref, c0, tk) for c0 in range(0, T, tk)]
        m = None
        for kr, vr, c0, n in chunks:
            sb, pb, mb = s_bufs[buf], p_bufs[buf], m_bufs[buf]
            buf = 1 - buf
            sb[:, 0:n] = lax.dot_general(q, kr[c0:c0 + n, qc], nt, preferred_element_type=F32)
            m_new = jnp.max(sb[:, 0:n], axis=-1, keepdims=True)
            if m is not None:
                m_new = jnp.maximum(m, m_new)
                alpha = jnp.exp2(m - m_new)
            mb[...] = jnp.broadcast_to(m_new, (tq, LANES))
            for r0 in range(0, tq, rb):
                mrow = mb[r0:r0 + rb, :]
                for cb in range(0, n, LANES):
                    p = jnp.exp2(sb[r0:r0 + rb, cb:cb + LANES] - mrow)
                    pb[r0:r0 + rb, cb:cb + LANES] = p.astype(BF16)
            pv = jnp.dot(pb[:, 0:n], vr[c0:c0 + n, vc], preferred_element_type=F32)
            acc[...] = pv if m is None else alpha * acc[...] + pv
            m = m_new
        a = acc[...]
        o_ref[:, hh * dv:(hh + 1) * dv] = (a[:, 0:dv] / a[:, dv:2 * dv]).astype(o_ref.dtype)


def _attention(qp, kp, v, tok0, B, T, ctx=None, nh=1, tq=256, tk=512, rb=64):
    H, dv = MLA_H, MLA_DV
    s0 = tok0 // T
    q0 = tok0 // tq
    nq = T // tq
    in_specs = [
        pl.BlockSpec((tq, nh * QK_PAD), lambda b, h, i: (q0 + b * nq + i, h)),
        pl.BlockSpec((T, nh * QK_PAD), lambda b, h, i: (s0 + b, h)),
        pl.BlockSpec((T, nh * V_PAD), lambda b, h, i: (s0 + b, h)),
    ]
    args = [qp, kp, v]
    width = min(tk, T)
    if ctx is not None:
        past = ctx[0].shape[0] // B
        width = max(width, past)
        in_specs += [pl.BlockSpec((past, nh * QK_PAD), lambda b, h, i: (b, h)),
                     pl.BlockSpec((past, nh * V_PAD), lambda b, h, i: (b, h))]
        args += list(ctx)
    s_buf = pltpu.VMEM((tq, width), F32)
    p_buf = pltpu.VMEM((tq, width), BF16)
    m_buf = pltpu.VMEM((tq, LANES), F32)
    return pl.pallas_call(
        functools.partial(_attn_kernel, nh=nh, has_ctx=ctx is not None, tk=tk, rb=rb),
        out_shape=jax.ShapeDtypeStruct((B * T, H * dv), BF16),
        grid=(B, H // nh, nq),
        in_specs=in_specs,
        out_specs=pl.BlockSpec((tq, nh * dv), lambda b, h, i: (b * nq + i, h)),
        scratch_shapes=[s_buf, s_buf, p_buf, p_buf, m_buf, m_buf, pltpu.VMEM((tq, V_PAD), F32)],
        compiler_params=_params("parallel", "parallel", "arbitrary"),
        name="attention_ctx" if ctx is not None else "attention",
    )(*args)


def _route(h, wa_ref, wb_ref, br_ref):
    n = N_EXPERTS
    nt = (((1,), (1,)), ((), ()))
    h_hi = h.astype(BF16)
    h_lo = (h - h_hi.astype(F32)).astype(BF16)
    a = lax.dot_general(wa_ref[...], h_hi, nt, preferred_element_type=F32)
    b = lax.dot_general(wb_ref[...], h_lo, nt, preferred_element_type=F32)
    logits = a[0:n, :] + a[n:2 * n, :] + b[0:n, :] + br_ref[...]
    sub = lax.broadcasted_iota(jnp.int32, logits.shape, 0).astype(F32)
    neg = jnp.float32(-jnp.inf)

    def top(vals):
        m = jnp.max(vals, axis=0, keepdims=True)
        return m, jnp.min(jnp.where(vals == m, sub, float(n)), axis=0, keepdims=True)

    m1, i1 = top(logits)
    m2, i2 = top(jnp.where(sub == i1, neg, logits))
    e = jnp.exp(m2 - m1)
    g1 = 1.0 / (1.0 + e)
    g2 = e / (1.0 + e)
    rec = jnp.where(sub == 0, i1, 0.0)
    rec = jnp.where(sub == 1, i2, rec)
    rec = jnp.where(sub == 2, g1, rec)
    return jnp.where(sub == 3, g2, rec)


def _outproj_kernel(yc_ref, yr_ref, yma_ref, ymb_ref, w_ref, x_ref, gate_ref, n1_ref, n2_ref,
                    sc_ref, sh_ref, *rest, routed, n_a):
    if routed:
        wa_ref, wb_ref, br_ref, x1_ref, h_ref, r_ref = rest
    else:
        x1_ref, h_ref = rest
    c1 = yc_ref.shape[1]
    c2 = c1 + yr_ref.shape[1]
    ym = jnp.where(pl.program_id(0) < n_a, yma_ref[...], ymb_ref[...])
    y = jnp.dot(yc_ref[...], w_ref[0:c1, :], preferred_element_type=F32)
    y = y + jnp.dot(yr_ref[...], w_ref[c1:c2, :], preferred_element_type=F32)
    y = y + jnp.dot(ym, w_ref[c2:, :], preferred_element_type=F32)
    x1 = x_ref[...] + gate_ref[...] * _rms(y, n1_ref[...])
    x1_ref[...] = x1
    h = _rms(x1, n2_ref[...]) * (1.0 + sc_ref[...]) + sh_ref[...]
    h_ref[...] = h.astype(h_ref.dtype)
    if routed:
        r_ref[...] = _route(h, wa_ref, wb_ref, br_ref)


def _outproj(yc, yr, yma, ymb, w_out, x, gains, mod, seg, router=None, tm=512):
    tok, D = x.shape
    routed = router is not None
    n_a = yma.shape[0] // tm
    row = lambda a: pl.BlockSpec((tm, a.shape[1]), lambda i: (i, 0))
    full = lambda a: pl.BlockSpec(a.shape, lambda i: (0,) * a.ndim)
    part_a = pl.BlockSpec((tm, yma.shape[1]), lambda i: (jnp.minimum(i, n_a - 1), 0))
    part_b = pl.BlockSpec((tm, ymb.shape[1]), lambda i: (jnp.maximum(i - n_a, 0), 0))
    in_specs = [row(yc), row(yr), part_a, part_b, full(w_out), row(x), _mod_spec(2, tm, seg),
                _gain_spec(1), _gain_spec(2), _mod_spec(4, tm, seg), _mod_spec(3, tm, seg)]
    args = [yc, yr, yma, ymb, w_out, x, mod, gains, gains, mod, mod]
    out_shape = [jax.ShapeDtypeStruct((tok, D), F32),
                 jax.ShapeDtypeStruct((tok, D), F32 if routed else BF16)]
    out_specs = [pl.BlockSpec((tm, D), lambda i: (i, 0)), pl.BlockSpec((tm, D), lambda i: (i, 0))]
    if routed:
        in_specs += [full(a) for a in router]
        args += list(router)
        out_shape.append(jax.ShapeDtypeStruct((SUBLANES, tok), F32))
        out_specs.append(pl.BlockSpec((SUBLANES, tm), lambda i: (0, i)))
    return pl.pallas_call(
        functools.partial(_outproj_kernel, routed=routed, n_a=n_a),
        out_shape=tuple(out_shape),
        grid=(tok // tm,),
        in_specs=in_specs,
        out_specs=tuple(out_specs),
        compiler_params=_params("parallel"),
        name="outproj_routed" if routed else "outproj",
    )(*args)


def _ffn_kernel(h_ref, wg_ref, wu_ref, wd_ref, x_ref, gate_ref, n_ref, o_ref, acc):
    f = pl.program_id(1)

    @pl.when(f == 0)
    def _():
        acc[...] = jnp.zeros_like(acc)

    h = h_ref[...]
    g = jnp.dot(h, wg_ref[...], preferred_element_type=F32)
    u = jnp.dot(h, wu_ref[...], preferred_element_type=F32)
    acc[...] += jnp.dot((_silu(g) * u).astype(BF16), wd_ref[...], preferred_element_type=F32)

    @pl.when(f == pl.num_programs(1) - 1)
    def _():
        o_ref[...] = x_ref[...] + gate_ref[...] * _rms(acc[...], n_ref[...])


def _dense_ffn(h, wg, wu, wd, x, gains, mod, seg, tm=512, tf=512):
    tok, D = x.shape
    F = wg.shape[1]
    return pl.pallas_call(
        _ffn_kernel,
        out_shape=jax.ShapeDtypeStruct((tok, D), F32),
        grid=(tok // tm, F // tf),
        in_specs=[
            pl.BlockSpec((tm, D), lambda i, f: (i, 0)),
            pl.BlockSpec((D, tf), lambda i, f: (0, f)),
            pl.BlockSpec((D, tf), lambda i, f: (0, f)),
            pl.BlockSpec((tf, D), lambda i, f: (f, 0)),
            pl.BlockSpec((tm, D), lambda i, f: (i, 0)),
            _mod_spec(5, tm, seg),
            _gain_spec(3),
        ],
        out_specs=pl.BlockSpec((tm, D), lambda i, f: (i, 0)),
        scratch_shapes=[pltpu.VMEM((tm, D), F32)],
        compiler_params=_params("parallel", "arbitrary"),
        name="dense_ffn",
    )(h, wg, wu, wd, x, mod, gains)


def _moe_ffn_kernel(te_ref, nu_ref, tok_ref, h_ref, wg_ref, wu_ref, wd_ref, o_ref, stage, xb, acc,
                    sem, *, tm, chunk):
    i = pl.program_id(0)
    f = pl.program_id(1)
    nf = pl.num_programs(1)
    nu = nu_ref[0]
    used = i < nu
    slot = i % 2

    def request(tile, sl, lo, count):
        def body(r, carry):
            pltpu.make_async_copy(h_ref.at[pl.ds(tok_ref[tile * tm + lo + r], 1), :],
                                  stage.at[sl, pl.ds(lo + r, 1), :], sem.at[sl]).start()
            return carry
        lax.fori_loop(0, count, body, 0, unroll=SUBLANES)

    @pl.when((i == 0) & (f == 0))
    def _():
        request(0, 0, 0, tm)

    @pl.when(used & (f == 0))
    def _():
        pltpu.make_async_copy(h_ref.at[pl.ds(0, tm), :], stage.at[slot], sem.at[slot]).wait()
        xb[...] = stage[slot].astype(BF16)
        acc[...] = jnp.zeros_like(acc)

    full_steps = tm // chunk

    @pl.when((i + 1 < nu) & (f < full_steps))
    def _():
        request(i + 1, 1 - slot, f * chunk, chunk)

    if tm % chunk:
        @pl.when((i + 1 < nu) & (f == full_steps))
        def _():
            request(i + 1, 1 - slot, full_steps * chunk, tm % chunk)

    @pl.when(used)
    def _():
        x = xb[...]
        g = jnp.dot(x, wg_ref[...], preferred_element_type=F32)
        u = jnp.dot(x, wu_ref[...], preferred_element_type=F32)
        acc[...] += jnp.dot((_silu(g) * u).astype(BF16), wd_ref[...], preferred_element_type=F32)

    @pl.when(used & (f == nf - 1))
    def _():
        o_ref[...] = acc[...]

    @pl.when(jnp.logical_not(used) & (f == nf - 1))
    def _():
        o_ref[...] = jnp.zeros_like(o_ref)


def _moe_ffn(h, slot_token, tile_expert, n_used, wg, wu, wd, tm, tf=1024):
    D = h.shape[1]
    n_slots = slot_token.shape[0]
    F = wg.shape[2]
    nf = F // tf
    n_tiles = n_slots // tm

    def fi(i, f, nu):
        return jnp.where(i < nu[0], f, nf - 1)

    chunk = pl.cdiv(pl.cdiv(tm, nf), SUBLANES) * SUBLANES
    assert pl.cdiv(tm, chunk) <= nf and tm % SUBLANES == 0
    return pl.pallas_call(
        functools.partial(_moe_ffn_kernel, tm=tm, chunk=chunk),
        out_shape=jax.ShapeDtypeStruct((n_slots, D), F32),
        grid_spec=pltpu.PrefetchScalarGridSpec(
            num_scalar_prefetch=3,
            grid=(n_tiles, nf),
            in_specs=[
                pl.BlockSpec(memory_space=pl.ANY),
                pl.BlockSpec((None, D, tf), lambda i, f, te, nu, tk: (te[i], 0, fi(i, f, nu))),
                pl.BlockSpec((None, D, tf), lambda i, f, te, nu, tk: (te[i], 0, fi(i, f, nu))),
                pl.BlockSpec((None, tf, D), lambda i, f, te, nu, tk: (te[i], fi(i, f, nu), 0)),
            ],
            out_specs=pl.BlockSpec((tm, D), lambda i, f, te, nu, tk: (i, 0)),
            scratch_shapes=[pltpu.VMEM((2, tm, D), F32), pltpu.VMEM((tm, D), BF16),
                            pltpu.VMEM((tm, D), F32), pltpu.SemaphoreType.DMA((2,))],
        ),
        compiler_params=_params("arbitrary", "arbitrary"),
        name="moe_ffn",
    )(tile_expert, n_used, slot_token, h, wg, wu, wd)


def _combine_kernel(d_ref, ys_ref, gates_ref, x_ref, gate_ref, n_ref, oa_ref, ob_ref, b0, b1, sem,
                    *, rows, n_a):
    i = pl.program_id(0)
    n = pl.num_programs(0)
    slot = i % 2

    def request(step, sl):
        def body(r, carry):
            a = 2 * (step * rows + r)
            pltpu.make_async_copy(ys_ref.at[pl.ds(d_ref[a], 1), :],
                                  b0.at[sl, pl.ds(r, 1), :], sem.at[0, sl]).start()
            pltpu.make_async_copy(ys_ref.at[pl.ds(d_ref[a + 1], 1), :],
                                  b1.at[sl, pl.ds(r, 1), :], sem.at[1, sl]).start()
            return carry
        lax.fori_loop(0, rows, body, 0, unroll=SUBLANES)

    @pl.when(i == 0)
    def _():
        request(0, 0)

    @pl.when(i + 1 < n)
    def _():
        request(i + 1, 1 - slot)

    pltpu.make_async_copy(ys_ref.at[pl.ds(0, rows), :], b0.at[slot], sem.at[0, slot]).wait()
    pltpu.make_async_copy(ys_ref.at[pl.ds(0, rows), :], b1.at[slot], sem.at[1, slot]).wait()
    gates = gates_ref[...]
    out = gates[:, 2:3] * b0[slot] + gates[:, 3:4] * b1[slot]
    res = x_ref[...] + gate_ref[...] * _rms(out, n_ref[...])

    @pl.when(i < n_a)
    def _():
        oa_ref[...] = res

    @pl.when(i >= n_a)
    def _():
        ob_ref[...] = res


def _moe_combine(ys, dest, route, x, gains, mod, seg, n_a_tok, rows=256):
    tok, D = x.shape
    n_a = n_a_tok // rows
    return pl.pallas_call(
        functools.partial(_combine_kernel, rows=rows, n_a=n_a),
        out_shape=(jax.ShapeDtypeStruct((n_a_tok, D), F32),
                   jax.ShapeDtypeStruct((tok - n_a_tok, D), F32)),
        grid_spec=pltpu.PrefetchScalarGridSpec(
            num_scalar_prefetch=1,
            grid=(tok // rows,),
            in_specs=[
                pl.BlockSpec(memory_space=pl.ANY),
                pl.BlockSpec((rows, SUBLANES), lambda i, d: (i, 0)),
                pl.BlockSpec((rows, D), lambda i, d: (i, 0)),
                _mod_spec(5, rows, seg),
                _gain_spec(3),
            ],
            out_specs=(pl.BlockSpec((rows, D), lambda i, d: (jnp.minimum(i, n_a - 1), 0)),
                       pl.BlockSpec((rows, D), lambda i, d: (jnp.maximum(i - n_a, 0), 0))),
            scratch_shapes=[pltpu.VMEM((2, rows, D), F32), pltpu.VMEM((2, rows, D), F32),
                            pltpu.SemaphoreType.DMA((2, 2))],
        ),
        compiler_params=_params("arbitrary"),
        name="moe_combine",
    )(dest, ys, route, x, mod, gains)


def _moe_plan(route, tm):
    tok = route.shape[0]
    n_assign = tok * TOP_K
    n_tiles = -(-n_assign // tm) + N_EXPERTS
    n_slots = n_tiles * tm
    expert = route[:, :TOP_K].astype(jnp.int32).reshape(-1)
    token = jnp.repeat(jnp.arange(tok, dtype=jnp.int32), TOP_K)
    onehot = (expert[:, None] == jnp.arange(N_EXPERTS, dtype=jnp.int32)[None, :]).astype(jnp.int32)
    csum = jnp.cumsum(onehot, axis=0)
    counts = csum[-1]
    padded = (counts + tm - 1) // tm * tm
    padded_end = jnp.cumsum(padded)
    padded_start = padded_end - padded
    dest = jnp.sum(onehot * (csum - 1 + padded_start[None, :]), axis=1).astype(jnp.int32)
    slot_token = jnp.zeros((n_slots,), jnp.int32).at[dest].set(token)
    n_used = (padded_end[-1] // tm).astype(jnp.int32)
    tile_start = jnp.minimum(jnp.arange(n_tiles, dtype=jnp.int32), n_used - 1) * tm
    tile_expert = jnp.minimum(
        jnp.sum((tile_start[:, None] >= padded_end[None, :]).astype(jnp.int32), axis=1),
        N_EXPERTS - 1)
    return slot_token, dest, tile_expert, n_used.reshape(1)


def _moe_layer(h, route, x, wg, wu, wd, gains, mod, seg, n_a_tok, tm=512):
    slot_token, dest, tile_expert, n_used = _moe_plan(route, tm)
    ys = _moe_ffn(h, slot_token, tile_expert, n_used, wg, wu, wd, tm)
    return _moe_combine(ys, dest, route, x, gains, mod, seg, n_a_tok)


def _rope_tables(n_tokens):
    half = ROPE_AXIS // 2
    t = jnp.arange(n_tokens)
    row = (t // GRID_W).astype(F32)
    col = (t % GRID_W).astype(F32)
    inv_freq = jnp.power(ROPE_BASE, -jnp.arange(0, ROPE_AXIS, 2, dtype=F32) / ROPE_AXIS)
    ar = row[:, None] * inv_freq
    ac = col[:, None] * inv_freq
    pad = LANES - MLA_DR
    cos = jnp.concatenate([jnp.cos(ar), jnp.cos(ar), jnp.cos(ac), jnp.cos(ac),
                           jnp.ones((n_tokens, pad), F32)], axis=1)
    sin = jnp.concatenate([-jnp.sin(ar), jnp.sin(ar), -jnp.sin(ac), jnp.sin(ac),
                           jnp.zeros((n_tokens, pad), F32)], axis=1)
    assert cos.shape[1] == LANES and half * 4 == MLA_DR
    ident = (jnp.ones_like(cos), jnp.zeros_like(sin))
    return jnp.stack([ident[0], cos]), jnp.stack([ident[1], sin])


def kernel(x_prompt, x_sample, cache_mla_ckv, cache_mla_kpe, state_ret, c, c_ctx, w_mod, b_mod, norm_gains, w_in, w_out, conv_w, conv_b, conv_ln_g, conv_ln_b, ret_decay_logit, ret_gn_g, mla_q_norm, mla_w_uq, mla_kv_norm, mla_w_ukv, ffn_w_gate, ffn_w_up, ffn_w_down, moe_w_router, moe_b_router, moe_w_gate, moe_w_up, moe_w_down):
    D = D_MODEL
    n_p = BATCH * SEQ
    n_s = DEC_BATCH * DEC_SEQ
    seg = DEC_SEQ
    assert n_p == seg, "context tokens must fill exactly one conditioning segment"
    n_seg = 1 + DEC_BATCH

    x = jnp.concatenate([x_prompt.reshape(n_p, D), x_sample.reshape(n_s, D)], axis=0)
    cond8 = jnp.concatenate([c_ctx[None, :], c, jnp.zeros((8 - n_seg, D), F32)], axis=0)
    mod = _modulation(cond8, w_mod, b_mod).reshape(DEPTH, 8, 6, 1, D)
    gains = norm_gains.reshape(DEPTH, 4, 1, D)
    rope_cos, rope_sin = _rope_tables(DEC_SEQ)
    kpe_pad = jnp.concatenate(
        [cache_mla_kpe, jnp.zeros(cache_mla_kpe.shape[:-1] + (LANES - MLA_DR,), F32)], axis=-1)

    ckv_layers, kpe_layers, ret_layers = [], [], []
    for l in range(DEPTH):
        w_in_p = jnp.concatenate([w_in[l], jnp.zeros((D, ZW - IN_COLS), F32)], axis=1).astype(BF16)
        z = _inproj(x, gains[l], mod[l], w_in_p, seg)

        y_conv = _conv_module(z, conv_w[l], conv_b[l], conv_ln_g[l], conv_ln_b[l], n_p, SEQ,
                              DEC_SEQ)

        s0_all = jnp.concatenate(
            [jnp.zeros((BATCH, 2, RET_H, RET_DK, RET_DV), F32), state_ret[:, l]], axis=0)
        y_ret, s_fin = _retention(z, ret_decay_logit[l], ret_gn_g[l], s0_all, n_p, SEQ, DEC_SEQ)

        wq, wk, wv = _mla_weights(mla_w_uq[l], mla_w_ukv[l])
        qp, kp, v, ckv_n = _mla_prep(z, rope_cos, rope_sin, mla_q_norm[l], mla_kv_norm[l],
                                     wq, wk, wv, seg)
        k_ctx, v_ctx = _ctx_prep(cache_mla_ckv[:, l].reshape(DEC_BATCH * PAST_LEN, MLA_KV_LORA),
                                 kpe_pad[:, l].reshape(DEC_BATCH * PAST_LEN, LANES), wk, wv)
        y_mla_a = _attention(qp, kp, v, 0, BATCH, SEQ, nh=MLA_H, tq=SEQ)
        y_mla_b = _attention(qp, kp, v, n_p, DEC_BATCH, DEC_SEQ, ctx=(k_ctx, v_ctx), tq=1024,
                             tk=1024)

        ckv_layers.append(ckv_n[:n_p].reshape(BATCH, SEQ, MLA_KV_LORA))
        kpe_layers.append(z[:n_p, ZC_KR:ZC_KR + MLA_DR].reshape(BATCH, SEQ, MLA_DR))
        ret_layers.append(s_fin[:BATCH])

        w_out_b = w_out[l].astype(BF16)
        i = l // 2
        if l % 2 == 0:
            x1, h = _outproj(y_conv, y_ret, y_mla_a, y_mla_b, w_out_b, x, gains[l], mod[l], seg)
            x = _dense_ffn(h, ffn_w_gate[i].astype(BF16), ffn_w_up[i].astype(BF16),
                           ffn_w_down[i].astype(BF16), x1, gains[l], mod[l], seg)
        else:
            wr = moe_w_router[i].T
            wr_hi = wr.astype(BF16)
            wr_lo = (wr - wr_hi.astype(F32)).astype(BF16)
            router = (jnp.concatenate([wr_hi, wr_lo], axis=0),
                      jnp.concatenate([wr_hi, jnp.zeros_like(wr_hi)], axis=0),
                      moe_b_router[i][:, None])
            x1, h, route_t = _outproj(y_conv, y_ret, y_mla_a, y_mla_b, w_out_b, x, gains[l], mod[l],
                                      seg, router=router)
            route = route_t.T
            parts = _moe_layer(h, route, x1, moe_w_gate[i].astype(BF16), moe_w_up[i].astype(BF16),
                               moe_w_down[i].astype(BF16), gains[l], mod[l], seg, n_p)
            x = jnp.concatenate(parts, axis=0) if l + 1 < DEPTH else None
        if l + 1 == DEPTH and l % 2 == 0:
            parts = (x[:n_p], x[n_p:])

    y_prompt = parts[0].reshape(BATCH, SEQ, D)
    y_sample = parts[1].reshape(DEC_BATCH, DEC_SEQ, D)
    return (y_prompt, y_sample, jnp.stack(ckv_layers, axis=1), jnp.stack(kpe_layers, axis=1),
            jnp.stack(ret_layers, axis=1))
```

```python
import functools

import jax
import jax.numpy as jnp
from jax import lax
from jax.experimental import pallas as pl
from jax.experimental.pallas import tpu as pltpu

F32 = jnp.float32
BF16 = jnp.bfloat16

D_MODEL = 2048
BATCH = 16
SEQ = 256
DEPTH = 2
DEC_BATCH = 4
DEC_SEQ = 4096
PAST_LEN = 256
GRID_W = 64
EPS = 1e-6
CONV_C = 512
CONV_K = 31
RET_H = 4
RET_DK = 128
RET_DV = 128
MLA_H = 8
MLA_DN = 128
MLA_DR = 64
MLA_DV = 128
MLA_Q_LORA = 768
MLA_KV_LORA = 256
MLA_SCALE = (MLA_DN + MLA_DR) ** -0.5
LOG2E = 1.4426950408889634
ROPE_BASE = 10000.0
ROPE_AXIS = MLA_DR // 2
D_FF = 5632
N_EXPERTS = 8
TOP_K = 2
D_EXPERT = 7168

LANES = 128
SUBLANES = 8
HALO = 16
QK_PAD = 256
V_PAD = 256

ZC_CONV = 0
ZC_RQ = 2 * CONV_C
ZC_RK = ZC_RQ + RET_H * RET_DK
ZC_RV = ZC_RK + RET_H * RET_DK
ZC_RG = ZC_RV + RET_H * RET_DV
ZC_CQ = ZC_RG + RET_H * RET_DV
ZC_CKV = ZC_CQ + MLA_Q_LORA
ZC_KR = ZC_CKV + MLA_KV_LORA
IN_COLS = ZC_KR + MLA_DR
ZW = 4608


def _params(*sem):
    return pltpu.CompilerParams(dimension_semantics=sem)


def _silu(x):
    return x * jax.nn.sigmoid(x)


def _rms(x, g):
    return x * lax.rsqrt(jnp.mean(x * x, axis=-1, keepdims=True) + EPS) * g


def _mod_kernel(c_ref, w_ref, b_ref, o_ref):
    c = c_ref[...]
    a = _silu(c).astype(BF16)
    o_ref[...] = jnp.dot(a, w_ref[...].astype(BF16), preferred_element_type=F32) + b_ref[...]


def _modulation(cond8, w_mod, b_mod, tn=1024):
    L, D, N = w_mod.shape
    return pl.pallas_call(
        _mod_kernel,
        out_shape=jax.ShapeDtypeStruct((L, 8, N), F32),
        grid=(L, N // tn),
        in_specs=[
            pl.BlockSpec((8, D), lambda l, j: (0, 0)),
            pl.BlockSpec((None, D, tn), lambda l, j: (l, 0, j)),
            pl.BlockSpec((None, 1, tn), lambda l, j: (l, 0, j)),
        ],
        out_specs=pl.BlockSpec((None, 8, tn), lambda l, j: (l, 0, j)),
        compiler_params=_params("parallel", "arbitrary"),
        name="modulation",
    )(cond8, w_mod, b_mod.reshape(L, 1, N))


def _inproj_kernel(x_ref, g_ref, sc_ref, sh_ref, w_ref, o_ref, h_scr, r_scr, *, rb):
    @pl.when(pl.program_id(1) == 0)
    def _():
        x = x_ref[...]
        r = lax.rsqrt(jnp.mean(x * x, axis=-1, keepdims=True) + EPS)
        r_scr[...] = jnp.broadcast_to(r, r_scr.shape)
        a = g_ref[...] * (1.0 + sc_ref[...])
        b = sh_ref[...]
        for r0 in range(0, x_ref.shape[0], rb):
            rr = r_scr[r0:r0 + rb, :]
            for c0 in range(0, x_ref.shape[1], LANES):
                cols = slice(c0, c0 + LANES)
                y = x_ref[r0:r0 + rb, cols] * rr * a[:, cols] + b[:, cols]
                h_scr[r0:r0 + rb, cols] = y.astype(BF16)

    o_ref[...] = jnp.dot(h_scr[...], w_ref[...], preferred_element_type=F32)


def _mod_spec(chunk, tm, seg):
    return pl.BlockSpec((None, None, 1, D_MODEL), lambda i, *_: (i // (seg // tm), chunk, 0, 0))


def _gain_spec(k):
    return pl.BlockSpec((None, 1, D_MODEL), lambda i, *_: (k, 0, 0))


def _inproj(x, gains, mod, w_in_p, seg, tm=1024, tn=1536, rb=64):
    tok, D = x.shape
    zw = w_in_p.shape[1]
    return pl.pallas_call(
        functools.partial(_inproj_kernel, rb=rb),
        out_shape=jax.ShapeDtypeStruct((tok, zw), F32),
        grid=(tok // tm, zw // tn),
        in_specs=[
            pl.BlockSpec((tm, D), lambda i, j: (i, 0)),
            _gain_spec(0),
            _mod_spec(1, tm, seg),
            _mod_spec(0, tm, seg),
            pl.BlockSpec((D, tn), lambda i, j: (0, j)),
        ],
        out_specs=pl.BlockSpec((tm, tn), lambda i, j: (i, j)),
        scratch_shapes=[pltpu.VMEM((tm, D), BF16), pltpu.VMEM((tm, LANES), F32)],
        compiler_params=_params("parallel", "arbitrary"),
        name="inproj",
    )(x, gains, mod, mod, w_in_p)


def _seq_pos(t, n_a, per_a, per_b):
    in_a = t < n_a
    tb = t - n_a
    seq = jnp.where(in_a, t // per_a, n_a // per_a + tb // per_b)
    pos = jnp.where(in_a, t % per_a, tb % per_b)
    n = jnp.where(in_a, per_a, per_b)
    return seq, pos, n


def _conv_kernel(zc_ref, zp_ref, zn_ref, w_ref, b_ref, g_ref, bb_ref, o_ref, ubuf, cbuf, shifted,
                 *, tt, n_a, per_a, per_b, rows):
    _, pos, n = _seq_pos(pl.program_id(0), n_a, per_a, per_b)
    C = CONV_C

    def glu(z):
        return z[:, :C] * jax.nn.sigmoid(z[:, C:])

    ubuf[0:HALO, :] = jnp.where(pos == 0, 0.0, glu(zp_ref[...]))
    ubuf[HALO:HALO + tt, :] = glu(zc_ref[...])
    ubuf[HALO + tt:2 * HALO + tt, :] = jnp.where(pos == n - 1, 0.0, glu(zn_ref[...]))

    span = shifted.shape[1]
    for s in range(1, SUBLANES):
        shifted[s - 1] = ubuf[s:s + span, :]

    base = HALO - CONV_K // 2
    for cb in range(C // LANES):
        cols = slice(cb * LANES, (cb + 1) * LANES)
        w = w_ref[:, cols]
        bias = b_ref[:, cols]
        for rb in range(tt // rows):
            r0 = rb * rows
            acc = jnp.zeros((rows, LANES), F32)
            for k in range(CONV_K):
                s = (base + k) % SUBLANES
                a = r0 + base + k - s
                tap = ubuf[a:a + rows, cols] if s == 0 else shifted[s - 1, a:a + rows, cols]
                acc = acc + tap * w[k:k + 1, :]
            cbuf[r0:r0 + rows, cols] = acc + bias

    u = cbuf[...]
    mu = jnp.mean(u, axis=-1, keepdims=True)
    d = u - mu
    var = jnp.mean(d * d, axis=-1, keepdims=True)
    y = d * lax.rsqrt(var + EPS) * g_ref[...] + bb_ref[...]
    o_ref[...] = _silu(y).astype(o_ref.dtype)


def _conv_module(z, conv_w, conv_b, ln_g, ln_b, n_a_tok, len_a, len_b, tt=256, rows=64):
    tok = z.shape[0]
    C = CONV_C
    hb = tt // HALO
    nhb = tok // HALO
    w_p = jnp.concatenate([conv_w, jnp.zeros((32 - CONV_K, C), F32)], axis=0)
    kern = functools.partial(_conv_kernel, tt=tt, n_a=n_a_tok // tt, per_a=len_a // tt,
                             per_b=len_b // tt, rows=rows)
    vec = pl.BlockSpec((1, C), lambda i: (0, 0))
    return pl.pallas_call(
        kern,
        out_shape=jax.ShapeDtypeStruct((tok, C), BF16),
        grid=(tok // tt,),
        in_specs=[
            pl.BlockSpec((tt, 2 * C), lambda i: (i, 0)),
            pl.BlockSpec((HALO, 2 * C), lambda i: (jnp.maximum(i * hb - 1, 0), 0)),
            pl.BlockSpec((HALO, 2 * C), lambda i: (jnp.minimum((i + 1) * hb, nhb - 1), 0)),
            pl.BlockSpec((32, C), lambda i: (0, 0)),
            vec, vec, vec,
        ],
        out_specs=pl.BlockSpec((tt, C), lambda i: (i, 0)),
        scratch_shapes=[pltpu.VMEM((tt + 2 * HALO, C), F32), pltpu.VMEM((tt, C), F32),
                        pltpu.VMEM((SUBLANES - 1, tt + 2 * HALO - SUBLANES, C), F32)],
        compiler_params=_params("parallel"),
        name="conv_module",
    )(z, z, z, w_p, conv_b.reshape(1, C), ln_g.reshape(1, C), ln_b.reshape(1, C))


def _log_sigmoid(x):
    return -(jnp.maximum(-x, 0.0) + jnp.log(1.0 + jnp.exp(-jnp.abs(x))))


def _ret_kernel(q_ref, k_ref, v_ref, dl_ref, s0_ref, *rest, C, n_a, per_a, per_b, n_chunks,
                backward):
    if backward:
        of_ref, rg_ref, gn_ref, y_ref, sfin_ref, S, dmat, qd, kd = rest
        c = n_chunks - 1 - pl.program_id(0)
    else:
        o_ref, sfin_ref, S, dmat, qd, kd = rest
        c = pl.program_id(0)
    _, pos, n = _seq_pos(c, n_a, per_a, per_b)
    first = (pos == n - 1) if backward else (pos == 0)
    last = (pos == 0) if backward else (pos == n - 1)
    H, dk, dv = RET_H, RET_DK, RET_DV

    @pl.when(pl.program_id(0) == 0)
    def _():
        row = lax.broadcasted_iota(jnp.int32, (C, LANES), 0).astype(F32)
        col = lax.broadcasted_iota(jnp.int32, (C, LANES), 1).astype(F32)
        for h in range(H):
            lg = _log_sigmoid(dl_ref[h])[0:1, :]
            if backward:
                qd[h] = jnp.exp((C - row) * lg)
                kd[h] = jnp.exp(row * lg)
            else:
                qd[h] = jnp.exp((row + 1.0) * lg)
                kd[h] = jnp.exp((C - 1.0 - row) * lg)
            for cb in range(C // LANES):
                diff = row - (col + cb * LANES)
                if backward:
                    diff = -diff
                dm = jnp.where(diff >= 0, jnp.exp(jnp.maximum(diff, 0.0) * lg), 0.0)
                dmat[h, :, cb * LANES:(cb + 1) * LANES] = dm

    @pl.when(first)
    def _():
        S[...] = s0_ref[...]

    nt = (((1,), (1,)), ((), ()))
    tn = (((0,), (0,)), ((), ()))
    for h in range(H):
        q = q_ref[:, h * dk:(h + 1) * dk].astype(BF16)
        kf = k_ref[:, h * dk:(h + 1) * dk] * (dk ** -0.5)
        v = v_ref[:, h * dv:(h + 1) * dv].astype(BF16)
        lg = _log_sigmoid(dl_ref[h])[0:1, :]
        att = lax.dot_general(q, kf.astype(BF16), nt, preferred_element_type=F32) * dmat[h]
        s_old = S[h]
        o = jnp.dot(att.astype(BF16), v, preferred_element_type=F32)
        o = o + qd[h] * jnp.dot(q, s_old.astype(BF16), preferred_element_type=F32)
        kv = lax.dot_general((kf * kd[h]).astype(BF16), v, tn, preferred_element_type=F32)
        S[h] = jnp.exp(C * lg) * s_old + kv
        if backward:
            t = of_ref[:, h * dv:(h + 1) * dv] + o
            mu = jnp.mean(t, axis=-1, keepdims=True)
            d = t - mu
            var = jnp.mean(d * d, axis=-1, keepdims=True)
            yn = d * lax.rsqrt(var + EPS) * gn_ref[:, h * dv:(h + 1) * dv]
            y_ref[:, h * dv:(h + 1) * dv] = (yn * _silu(rg_ref[:, h * dv:(h + 1) * dv])).astype(
                y_ref.dtype)
        else:
            o_ref[:, h * dv:(h + 1) * dv] = o

    @pl.when(last)
    def _():
        sfin_ref[...] = S[...]


def _retention(z, decay_logit, gn_g, s0_all, n_a_tok, len_a, len_b, C=256):
    tok = z.shape[0]
    H, dk, dv = RET_H, RET_DK, RET_DV
    W = H * dk
    n_chunks = tok // C
    n_a, per_a, per_b = n_a_tok // C, len_a // C, len_b // C
    n_seq = s0_all.shape[0]
    dl = jnp.broadcast_to(decay_logit[:, :, None, None], (2, H, 8, LANES)).astype(F32)
    geo = dict(C=C, n_a=n_a, per_a=per_a, per_b=per_b, n_chunks=n_chunks)

    def order(backward):
        return (lambda g: n_chunks - 1 - g) if backward else (lambda g: g)

    def zspec(col0, backward):
        cidx = col0 // W
        f = order(backward)
        return pl.BlockSpec((C, W), lambda g: (f(g), cidx))

    def seq_of(backward):
        f = order(backward)
        return lambda g: _seq_pos(f(g), n_a, per_a, per_b)[0]

    def common_in(d):
        b = d == 1
        s = seq_of(b)
        return [
            zspec(ZC_RQ, b), zspec(ZC_RK, b), zspec(ZC_RV, b),
            pl.BlockSpec((None, H, 8, LANES), lambda g: (d, 0, 0, 0)),
            pl.BlockSpec((None, None, H, dk, dv), lambda g: (s(g), d, 0, 0, 0)),
        ]

    def sfin_spec(d):
        s = seq_of(d == 1)
        return pl.BlockSpec((None, H, dk, dv), lambda g: (s(g), 0, 0, 0))

    scratch = [pltpu.VMEM((H, dk, dv), F32), pltpu.VMEM((H, C, C), F32),
               pltpu.VMEM((H, C, LANES), F32), pltpu.VMEM((H, C, LANES), F32)]
    sfin_shape = jax.ShapeDtypeStruct((n_seq, H, dk, dv), F32)

    o_f, s_f = pl.pallas_call(
        functools.partial(_ret_kernel, backward=False, **geo),
        out_shape=(jax.ShapeDtypeStruct((tok, W), F32), sfin_shape),
        grid=(n_chunks,),
        in_specs=common_in(0),
        out_specs=(pl.BlockSpec((C, W), lambda g: (g, 0)), sfin_spec(0)),
        scratch_shapes=scratch,
        compiler_params=_params("arbitrary"),
        name="retention_fwd",
    )(z, z, z, dl, s0_all)

    rev = order(True)
    y, s_b = pl.pallas_call(
        functools.partial(_ret_kernel, backward=True, **geo),
        out_shape=(jax.ShapeDtypeStruct((tok, W), BF16), sfin_shape),
        grid=(n_chunks,),
        in_specs=common_in(1) + [
            pl.BlockSpec((C, W), lambda g: (rev(g), 0)),
            zspec(ZC_RG, True),
            pl.BlockSpec((1, W), lambda g: (0, 0)),
        ],
        out_specs=(pl.BlockSpec((C, W), lambda g: (rev(g), 0)), sfin_spec(1)),
        scratch_shapes=scratch,
        compiler_params=_params("arbitrary"),
        name="retention_bwd",
    )(z, z, z, dl, s0_all, o_f, z, gn_g.reshape(1, W))
    return y, jnp.stack([s_f, s_b], axis=1)


def _rope(x, cos, sin):
    lane = lax.broadcasted_iota(jnp.int32, x.shape, 1)
    fwd = pltpu.roll(x, LANES - 16, axis=1)
    bwd = pltpu.roll(x, 16, axis=1)
    partner = jnp.where((lane % 32) < 16, fwd, bwd)
    return x * cos + partner * sin


def _mla_prep_kernel(cq_ref, ckv_ref, kr_ref, cos_ref, sin_ref, qg_ref, kg_ref, wq_ref, wk_ref,
                     wv_ref, q_out, k_out, v_out, ckvn_out):
    cos = cos_ref[...]
    sin = sin_ref[...]
    cqn = _rms(cq_ref[...], qg_ref[...]).astype(BF16)
    qa = jnp.dot(cqn, wq_ref[...], preferred_element_type=F32) * (MLA_SCALE * LOG2E)
    ckvn = _rms(ckv_ref[...], kg_ref[...])
    ckvn_out[...] = ckvn
    kb = ckvn.astype(BF16)
    for h in range(MLA_H):
        c0 = h * QK_PAD
        q_out[:, c0:c0 + LANES] = qa[:, c0:c0 + LANES].astype(BF16)
        q_out[:, c0 + LANES:c0 + QK_PAD] = _rope(qa[:, c0 + LANES:c0 + QK_PAD], cos, sin).astype(
            BF16)
    _store_kv(kb, _rope(kr_ref[...], cos, sin), wk_ref, wv_ref, k_out, v_out)


def _store_kv(kb, kr, wk_ref, wv_ref, k_out, v_out):
    kn = jnp.dot(kb, wk_ref[...], preferred_element_type=F32)
    vv = jnp.dot(kb, wv_ref[...], preferred_element_type=F32)
    kr = kr.astype(BF16)
    ones = jnp.ones((kb.shape[0], V_PAD - MLA_DV), BF16)
    for h in range(MLA_H):
        k_out[:, h * QK_PAD:h * QK_PAD + LANES] = kn[:, h * MLA_DN:(h + 1) * MLA_DN].astype(BF16)
        k_out[:, h * QK_PAD + LANES:(h + 1) * QK_PAD] = kr
        v_out[:, h * V_PAD:h * V_PAD + MLA_DV] = vv[:, h * MLA_DV:(h + 1) * MLA_DV].astype(BF16)
        v_out[:, h * V_PAD + MLA_DV:(h + 1) * V_PAD] = ones


def _ctx_prep_kernel(ckv_ref, kr_ref, wk_ref, wv_ref, k_out, v_out):
    _store_kv(ckv_ref[...].astype(BF16), kr_ref[...], wk_ref, wv_ref, k_out, v_out)


def _mla_weights(w_uq, w_ukv):
    H, dn, dr, dv = MLA_H, MLA_DN, MLA_DR, MLA_DV
    wq = w_uq.reshape(-1, H, dn + dr)
    wq = jnp.concatenate([wq, jnp.zeros(wq.shape[:2] + (QK_PAD - dn - dr,), wq.dtype)], axis=-1)
    wkv = w_ukv.reshape(-1, H, dn + dv)
    wk = wkv[..., :dn].reshape(-1, H * dn)
    wv = wkv[..., dn:].reshape(-1, H * dv)
    return wq.reshape(-1, H * QK_PAD).astype(BF16), wk.astype(BF16), wv.astype(BF16)


def _mla_prep(z, rope_cos, rope_sin, q_norm, kv_norm, wq, wk, wv, seg, tm=512):
    tok = z.shape[0]
    H = MLA_H
    per = seg // tm
    tab = pl.BlockSpec((None, tm, LANES), lambda i: (jnp.minimum(i // per, 1), i % per, 0))
    full = lambda a: pl.BlockSpec(a.shape, lambda i: (0,) * a.ndim)
    qg = q_norm.reshape(1, -1)
    kg = kv_norm.reshape(1, -1)
    return pl.pallas_call(
        _mla_prep_kernel,
        out_shape=(jax.ShapeDtypeStruct((tok, H * QK_PAD), BF16),
                   jax.ShapeDtypeStruct((tok, H * QK_PAD), BF16),
                   jax.ShapeDtypeStruct((tok, H * V_PAD), BF16),
                   jax.ShapeDtypeStruct((tok, MLA_KV_LORA), F32)),
        grid=(tok // tm,),
        in_specs=[
            pl.BlockSpec((tm, MLA_Q_LORA), lambda i: (i, ZC_CQ // MLA_Q_LORA)),
            pl.BlockSpec((tm, MLA_KV_LORA), lambda i: (i, ZC_CKV // MLA_KV_LORA)),
            pl.BlockSpec((tm, LANES), lambda i: (i, ZC_KR // LANES)),
            tab, tab, full(qg), full(kg), full(wq), full(wk), full(wv),
        ],
        out_specs=(pl.BlockSpec((tm, H * QK_PAD), lambda i: (i, 0)),
                   pl.BlockSpec((tm, H * QK_PAD), lambda i: (i, 0)),
                   pl.BlockSpec((tm, H * V_PAD), lambda i: (i, 0)),
                   pl.BlockSpec((tm, MLA_KV_LORA), lambda i: (i, 0))),
        compiler_params=_params("parallel"),
        name="mla_prep",
    )(z, z, z, rope_cos, rope_sin, qg, kg, wq, wk, wv)


def _ctx_prep(ckv_ctx, kpe_ctx, wk, wv, tm=256):
    rows = ckv_ctx.shape[0]
    H = MLA_H
    full = lambda a: pl.BlockSpec(a.shape, lambda i: (0,) * a.ndim)
    return pl.pallas_call(
        _ctx_prep_kernel,
        out_shape=(jax.ShapeDtypeStruct((rows, H * QK_PAD), BF16),
                   jax.ShapeDtypeStruct((rows, H * V_PAD), BF16)),
        grid=(rows // tm,),
        in_specs=[pl.BlockSpec((tm, MLA_KV_LORA), lambda i: (i, 0)),
                  pl.BlockSpec((tm, LANES), lambda i: (i, 0)), full(wk), full(wv)],
        out_specs=(pl.BlockSpec((tm, H * QK_PAD), lambda i: (i, 0)),
                   pl.BlockSpec((tm, H * V_PAD), lambda i: (i, 0))),
        compiler_params=_params("parallel"),
        name="ctx_prep",
    )(ckv_ctx, kpe_ctx, wk, wv)


def _attn_kernel(*refs, nh, has_ctx, tk, rb):
    if has_ctx:
        q_ref, k_ref, v_ref, kc_ref, vc_ref, o_ref, *scr = refs
    else:
        q_ref, k_ref, v_ref, o_ref, *scr = refs
    s_bufs, p_bufs, m_bufs, acc = scr[0:2], scr[2:4], scr[4:6], scr[6]
    nt = (((1,), (1,)), ((), ()))
    tq = q_ref.shape[0]
    T = k_ref.shape[0]
    tk = min(tk, T)
    dv = MLA_DV
    buf = 0
    for hh in range(nh):
        qc = slice(hh * QK_PAD, (hh + 1) * QK_PAD)
        vc = slice(hh * V_PAD, (hh + 1) * V_PAD)
        q = q_ref[:, qc]
        chunks = [(kc_ref, vc_ref, 0, kc_ref.shape[0])] if has_ctx else []
        chunks += [(k_ref, v_ref, c0, tk) for c0 in range(0, T, tk)]
        m = None
        for kr, vr, c0, n in chunks:
            sb, pb, mb = s_bufs[buf], p_bufs[buf], m_bufs[buf]
            buf = 1 - buf
            sb[:, 0:n] = lax.dot_general(q, kr[c0:c0 + n, qc], nt, preferred_element_type=F32)
            m_new = jnp.max(sb[:, 0:n], axis=-1, keepdims=True)
            if m is not None:
                m_new = jnp.maximum(m, m_new)
                alpha = jnp.exp2(m - m_new)
            mb[...] = jnp.broadcast_to(m_new, (tq, LANES))
            for r0 in range(0, tq, rb):
                mrow = mb[r0:r0 + rb, :]
                for cb in range(0, n, LANES):
                    p = jnp.exp2(sb[r0:r0 + rb, cb:cb + LANES] - mrow)
                    pb[r0:r0 + rb, cb:cb + LANES] = p.astype(BF16)
            pv = jnp.dot(pb[:, 0:n], vr[c0:c0 + n, vc], preferred_element_type=F32)
            acc[...] = pv if m is None else alpha * acc[...] + pv
            m = m_new
        a = acc[...]
        o_ref[:, hh * dv:(hh + 1) * dv] = (a[:, 0:dv] / a[:, dv:2 * dv]).astype(o_ref.dtype)


def _attention(qp, kp, v, tok0, B, T, ctx=None, nh=1, tq=256, tk=512, rb=64):
    H, dv = MLA_H, MLA_DV
    s0 = tok0 // T
    q0 = tok0 // tq
    nq = T // tq
    in_specs = [
        pl.BlockSpec((tq, nh * QK_PAD), lambda b, h, i: (q0 + b * nq + i, h)),
        pl.BlockSpec((T, nh * QK_PAD), lambda b, h, i: (s0 + b, h)),
        pl.BlockSpec((T, nh * V_PAD), lambda b, h, i: (s0 + b, h)),
    ]
    args = [qp, kp, v]
    width = min(tk, T)
    if ctx is not None:
        past = ctx[0].shape[0] // B
        width = max(width, past)
        in_specs += [pl.BlockSpec((past, nh * QK_PAD), lambda b, h, i: (b, h)),
                     pl.BlockSpec((past, nh * V_PAD), lambda b, h, i: (b, h))]
        args += list(ctx)
    s_buf = pltpu.VMEM((tq, width), F32)
    p_buf = pltpu.VMEM((tq, width), BF16)
    m_buf = pltpu.VMEM((tq, LANES), F32)
    return pl.pallas_call(
        functools.partial(_attn_kernel, nh=nh, has_ctx=ctx is not None, tk=tk, rb=rb),
        out_shape=jax.ShapeDtypeStruct((B * T, H * dv), BF16),
        grid=(B, H // nh, nq),
        in_specs=in_specs,
        out_specs=pl.BlockSpec((tq, nh * dv), lambda b, h, i: (b * nq + i, h)),
        scratch_shapes=[s_buf, s_buf, p_buf, p_buf, m_buf, m_buf, pltpu.VMEM((tq, V_PAD), F32)],
        compiler_params=_params("parallel", "parallel", "arbitrary"),
        name="attention_ctx" if ctx is not None else "attention",
    )(*args)


def _route(h, wa_ref, wb_ref, br_ref):
    n = N_EXPERTS
    nt = (((1,), (1,)), ((), ()))
    h_hi = h.astype(BF16)
    h_lo = (h - h_hi.astype(F32)).astype(BF16)
    a = lax.dot_general(wa_ref[...], h_hi, nt, preferred_element_type=F32)
    b = lax.dot_general(wb_ref[...], h_lo, nt, preferred_element_type=F32)
    logits = a[0:n, :] + a[n:2 * n, :] + b[0:n, :] + br_ref[...]
    sub = lax.broadcasted_iota(jnp.int32, logits.shape, 0).astype(F32)
    neg = jnp.float32(-jnp.inf)

    def top(vals):
        m = jnp.max(vals, axis=0, keepdims=True)
        return m, jnp.min(jnp.where(vals == m, sub, float(n)), axis=0, keepdims=True)

    m1, i1 = top(logits)
    m2, i2 = top(jnp.where(sub == i1, neg, logits))
    e = jnp.exp(m2 - m1)
    g1 = 1.0 / (1.0 + e)
    g2 = e / (1.0 + e)
    rec = jnp.where(sub == 0, i1, 0.0)
    rec = jnp.where(sub == 1, i2, rec)
    rec = jnp.where(sub == 2, g1, rec)
    return jnp.where(sub == 3, g2, rec)


def _outproj_kernel(yc_ref, yr_ref, yma_ref, ymb_ref, w_ref, x_ref, gate_ref, n1_ref, n2_ref,
                    sc_ref, sh_ref, *rest, routed, n_a):
    if routed:
        wa_ref, wb_ref, br_ref, x1_ref, h_ref, r_ref = rest
    else:
        x1_ref, h_ref = rest
    c1 = yc_ref.shape[1]
    c2 = c1 + yr_ref.shape[1]
    ym = jnp.where(pl.program_id(0) < n_a, yma_ref[...], ymb_ref[...])
    y = jnp.dot(yc_ref[...], w_ref[0:c1, :], preferred_element_type=F32)
    y = y + jnp.dot(yr_ref[...], w_ref[c1:c2, :], preferred_element_type=F32)
    y = y + jnp.dot(ym, w_ref[c2:, :], preferred_element_type=F32)
    x1 = x_ref[...] + gate_ref[...] * _rms(y, n1_ref[...])
    x1_ref[...] = x1
    h = _rms(x1, n2_ref[...]) * (1.0 + sc_ref[...]) + sh_ref[...]
    h_ref[...] = h.astype(h_ref.dtype)
    if routed:
        r_ref[...] = _route(h, wa_ref, wb_ref, br_ref)


def _outproj(yc, yr, yma, ymb, w_out, x, gains, mod, seg, router=None, tm=512):
    tok, D = x.shape
    routed = router is not None
    n_a = yma.shape[0] // tm
    row = lambda a: pl.BlockSpec((tm, a.shape[1]), lambda i: (i, 0))
    full = lambda a: pl.BlockSpec(a.shape, lambda i: (0,) * a.ndim)
    part_a = pl.BlockSpec((tm, yma.shape[1]), lambda i: (jnp.minimum(i, n_a - 1), 0))
    part_b = pl.BlockSpec((tm, ymb.shape[1]), lambda i: (jnp.maximum(i - n_a, 0), 0))
    in_specs = [row(yc), row(yr), part_a, part_b, full(w_out), row(x), _mod_spec(2, tm, seg),
                _gain_spec(1), _gain_spec(2), _mod_spec(4, tm, seg), _mod_spec(3, tm, seg)]
    args = [yc, yr, yma, ymb, w_out, x, mod, gains, gains, mod, mod]
    out_shape = [jax.ShapeDtypeStruct((tok, D), F32),
                 jax.ShapeDtypeStruct((tok, D), F32 if routed else BF16)]
    out_specs = [pl.BlockSpec((tm, D), lambda i: (i, 0)), pl.BlockSpec((tm, D), lambda i: (i, 0))]
    if routed:
        in_specs += [full(a) for a in router]
        args += list(router)
        out_shape.append(jax.ShapeDtypeStruct((SUBLANES, tok), F32))
        out_specs.append(pl.BlockSpec((SUBLANES, tm), lambda i: (0, i)))
    return pl.pallas_call(
        functools.partial(_outproj_kernel, routed=routed, n_a=n_a),
        out_shape=tuple(out_shape),
        grid=(tok // tm,),
        in_specs=in_specs,
        out_specs=tuple(out_specs),
        compiler_params=_params("parallel"),
        name="outproj_routed" if routed else "outproj",
    )(*args)


def _ffn_kernel(h_ref, wg_ref, wu_ref, wd_ref, x_ref, gate_ref, n_ref, o_ref, acc):
    f = pl.program_id(1)

    @pl.when(f == 0)
    def _():
        acc[...] = jnp.zeros_like(acc)

    h = h_ref[...]
    g = jnp.dot(h, wg_ref[...], preferred_element_type=F32)
    u = jnp.dot(h, wu_ref[...], preferred_element_type=F32)
    acc[...] += jnp.dot((_silu(g) * u).astype(BF16), wd_ref[...], preferred_element_type=F32)

    @pl.when(f == pl.num_programs(1) - 1)
    def _():
        o_ref[...] = x_ref[...] + gate_ref[...] * _rms(acc[...], n_ref[...])


def _dense_ffn(h, wg, wu, wd, x, gains, mod, seg, tm=512, tf=512):
    tok, D = x.shape
    F = wg.shape[1]
    return pl.pallas_call(
        _ffn_kernel,
        out_shape=jax.ShapeDtypeStruct((tok, D), F32),
        grid=(tok // tm, F // tf),
        in_specs=[
            pl.BlockSpec((tm, D), lambda i, f: (i, 0)),
            pl.BlockSpec((D, tf), lambda i, f: (0, f)),
            pl.BlockSpec((D, tf), lambda i, f: (0, f)),
            pl.BlockSpec((tf, D), lambda i, f: (f, 0)),
            pl.BlockSpec((tm, D), lambda i, f: (i, 0)),
            _mod_spec(5, tm, seg),
            _gain_spec(3),
        ],
        out_specs=pl.BlockSpec((tm, D), lambda i, f: (i, 0)),
        scratch_shapes=[pltpu.VMEM((tm, D), F32)],
        compiler_params=_params("parallel", "arbitrary"),
        name="dense_ffn",
    )(h, wg, wu, wd, x, mod, gains)


def _moe_ffn_kernel(te_ref, nu_ref, tok_ref, h_ref, wg_ref, wu_ref, wd_ref, o_ref, stage, xb, acc,
                    sem, *, tm, chunk):
    i = pl.program_id(0)
    f = pl.program_id(1)
    nf = pl.num_programs(1)
    nu = nu_ref[0]
    used = i < nu
    slot = i % 2
    n_stage = stage.shape[1]

    def request(tile, sl, r):
        src_row = tok_ref[tile * tm + jnp.minimum(r, tm - 1)]
        pltpu.make_async_copy(h_ref.at[pl.ds(src_row, 1), :], stage.at[sl, pl.ds(r, 1), :],
                              sem.at[sl]).start()

    def wait_all(sl):
        pltpu.make_async_copy(h_ref.at[pl.ds(0, n_stage), :], stage.at[sl], sem.at[sl]).wait()

    @pl.when((i == 0) & (f == 0))
    def _():
        def body(r, carry):
            request(0, 0, r)
            return carry
        lax.fori_loop(0, n_stage, body, 0, unroll=SUBLANES)

    @pl.when(used & (f == 0))
    def _():
        wait_all(slot)
        xb[...] = stage[slot, 0:tm, :].astype(BF16)
        acc[...] = jnp.zeros_like(acc)

    @pl.when(used)
    def _():
        nxt = jnp.minimum(i + 1, nu - 1)
        for k in range(chunk):
            request(nxt, 1 - slot, f * chunk + k)
        x = xb[...]
        g = jnp.dot(x, wg_ref[...], preferred_element_type=F32)
        u = jnp.dot(x, wu_ref[...], preferred_element_type=F32)
        acc[...] += jnp.dot((_silu(g) * u).astype(BF16), wd_ref[...], preferred_element_type=F32)

    @pl.when(used & (f == nf - 1))
    def _():
        o_ref[...] = acc[...]

    @pl.when((i + 1 == nu) & (f == nf - 1))
    def _():
        wait_all(1 - slot)

    @pl.when(jnp.logical_not(used) & (f == nf - 1))
    def _():
        o_ref[...] = jnp.zeros_like(o_ref)


def _moe_ffn(h, slot_token, tile_expert, n_used, wg, wu, wd, tm, tf=1024):
    D = h.shape[1]
    n_slots = slot_token.shape[0]
    F = wg.shape[2]
    nf = F // tf
    n_tiles = n_slots // tm

    def fi(i, f, nu):
        return jnp.where(i < nu[0], f, nf - 1)

    chunk = pl.cdiv(pl.cdiv(tm, nf), SUBLANES) * SUBLANES
    return pl.pallas_call(
        functools.partial(_moe_ffn_kernel, tm=tm, chunk=chunk),
        out_shape=jax.ShapeDtypeStruct((n_slots, D), F32),
        grid_spec=pltpu.PrefetchScalarGridSpec(
            num_scalar_prefetch=3,
            grid=(n_tiles, nf),
            in_specs=[
                pl.BlockSpec(memory_space=pl.ANY),
                pl.BlockSpec((None, D, tf), lambda i, f, te, nu, tk: (te[i], 0, fi(i, f, nu))),
                pl.BlockSpec((None, D, tf), lambda i, f, te, nu, tk: (te[i], 0, fi(i, f, nu))),
                pl.BlockSpec((None, tf, D), lambda i, f, te, nu, tk: (te[i], fi(i, f, nu), 0)),
            ],
            out_specs=pl.BlockSpec((tm, D), lambda i, f, te, nu, tk: (i, 0)),
            scratch_shapes=[pltpu.VMEM((2, nf * chunk, D), F32), pltpu.VMEM((tm, D), BF16),
                            pltpu.VMEM((tm, D), F32), pltpu.SemaphoreType.DMA((2,))],
        ),
        compiler_params=_params("arbitrary", "arbitrary"),
        name="moe_ffn",
    )(tile_expert, n_used, slot_token, h, wg, wu, wd)


def _combine_kernel(d_ref, ys_ref, gates_ref, x_ref, gate_ref, n_ref, oa_ref, ob_ref, b0, b1, sem,
                    *, rows, n_a):
    i = pl.program_id(0)
    n = pl.num_programs(0)
    slot = i % 2

    def request(step, sl):
        def body(r, carry):
            a = 2 * (step * rows + r)
            pltpu.make_async_copy(ys_ref.at[pl.ds(d_ref[a], 1), :],
                                  b0.at[sl, pl.ds(r, 1), :], sem.at[0, sl]).start()
            pltpu.make_async_copy(ys_ref.at[pl.ds(d_ref[a + 1], 1), :],
                                  b1.at[sl, pl.ds(r, 1), :], sem.at[1, sl]).start()
            return carry
        lax.fori_loop(0, rows, body, 0, unroll=SUBLANES)

    @pl.when(i == 0)
    def _():
        request(0, 0)

    @pl.when(i + 1 < n)
    def _():
        request(i + 1, 1 - slot)

    pltpu.make_async_copy(ys_ref.at[pl.ds(0, rows), :], b0.at[slot], sem.at[0, slot]).wait()
    pltpu.make_async_copy(ys_ref.at[pl.ds(0, rows), :], b1.at[slot], sem.at[1, slot]).wait()
    gates = gates_ref[...]
    out = gates[:, 2:3] * b0[slot] + gates[:, 3:4] * b1[slot]
    res = x_ref[...] + gate_ref[...] * _rms(out, n_ref[...])

    @pl.when(i < n_a)
    def _():
        oa_ref[...] = res

    @pl.when(i >= n_a)
    def _():
        ob_ref[...] = res


def _moe_combine(ys, dest, route, x, gains, mod, seg, n_a_tok, rows=256):
    tok, D = x.shape
    n_a = n_a_tok // rows
    return pl.pallas_call(
        functools.partial(_combine_kernel, rows=rows, n_a=n_a),
        out_shape=(jax.ShapeDtypeStruct((n_a_tok, D), F32),
                   jax.ShapeDtypeStruct((tok - n_a_tok, D), F32)),
        grid_spec=pltpu.PrefetchScalarGridSpec(
            num_scalar_prefetch=1,
            grid=(tok // rows,),
            in_specs=[
                pl.BlockSpec(memory_space=pl.ANY),
                pl.BlockSpec((rows, SUBLANES), lambda i, d: (i, 0)),
                pl.BlockSpec((rows, D), lambda i, d: (i, 0)),
                _mod_spec(5, rows, seg),
                _gain_spec(3),
            ],
            out_specs=(pl.BlockSpec((rows, D), lambda i, d: (jnp.minimum(i, n_a - 1), 0)),
                       pl.BlockSpec((rows, D), lambda i, d: (jnp.maximum(i - n_a, 0), 0))),
            scratch_shapes=[pltpu.VMEM((2, rows, D), F32), pltpu.VMEM((2, rows, D), F32),
                            pltpu.SemaphoreType.DMA((2, 2))],
        ),
        compiler_params=_params("arbitrary"),
        name="moe_combine",
    )(dest, ys, route, x, mod, gains)


def _moe_plan(route, tm):
    tok = route.shape[0]
    n_assign = tok * TOP_K
    n_tiles = -(-n_assign // tm) + N_EXPERTS
    n_slots = n_tiles * tm
    expert = route[:, :TOP_K].astype(jnp.int32).reshape(-1)
    token = jnp.repeat(jnp.arange(tok, dtype=jnp.int32), TOP_K)
    onehot = (expert[:, None] == jnp.arange(N_EXPERTS, dtype=jnp.int32)[None, :]).astype(jnp.int32)
    csum = jnp.cumsum(onehot, axis=0)
    counts = csum[-1]
    padded = (counts + tm - 1) // tm * tm
    padded_end = jnp.cumsum(padded)
    padded_start = padded_end - padded
    dest = jnp.sum(onehot * (csum - 1 + padded_start[None, :]), axis=1).astype(jnp.int32)
    slot_token = jnp.zeros((n_slots,), jnp.int32).at[dest].set(token)
    n_used = (padded_end[-1] // tm).astype(jnp.int32)
    tile_start = jnp.minimum(jnp.arange(n_tiles, dtype=jnp.int32), n_used - 1) * tm
    tile_expert = jnp.minimum(
        jnp.sum((tile_start[:, None] >= padded_end[None, :]).astype(jnp.int32), axis=1),
        N_EXPERTS - 1)
    return slot_token, dest, tile_expert, n_used.reshape(1)


def _moe_layer(h, route, x, wg, wu, wd, gains, mod, seg, n_a_tok, tm=512):
    slot_token, dest, tile_expert, n_used = _moe_plan(route, tm)
    ys = _moe_ffn(h, slot_token, tile_expert, n_used, wg, wu, wd, tm)
    return _moe_combine(ys, dest, route, x, gains, mod, seg, n_a_tok)


def _rope_tables(n_tokens):
    half = ROPE_AXIS // 2
    t = jnp.arange(n_tokens)
    row = (t // GRID_W).astype(F32)
    col = (t % GRID_W).astype(F32)
    inv_freq = jnp.power(ROPE_BASE, -jnp.arange(0, ROPE_AXIS, 2, dtype=F32) / ROPE_AXIS)
    ar = row[:, None] * inv_freq
    ac = col[:, None] * inv_freq
    pad = LANES - MLA_DR
    cos = jnp.concatenate([jnp.cos(ar), jnp.cos(ar), jnp.cos(ac), jnp.cos(ac),
                           jnp.ones((n_tokens, pad), F32)], axis=1)
    sin = jnp.concatenate([-jnp.sin(ar), jnp.sin(ar), -jnp.sin(ac), jnp.sin(ac),
                           jnp.zeros((n_tokens, pad), F32)], axis=1)
    assert cos.shape[1] == LANES and half * 4 == MLA_DR
    ident = (jnp.ones_like(cos), jnp.zeros_like(sin))
    return jnp.stack([ident[0], cos]), jnp.stack([ident[1], sin])


def kernel(x_prompt, x_sample, cache_mla_ckv, cache_mla_kpe, state_ret, c, c_ctx, w_mod, b_mod, norm_gains, w_in, w_out, conv_w, conv_b, conv_ln_g, conv_ln_b, ret_decay_logit, ret_gn_g, mla_q_norm, mla_w_uq, mla_kv_norm, mla_w_ukv, ffn_w_gate, ffn_w_up, ffn_w_down, moe_w_router, moe_b_router, moe_w_gate, moe_w_up, moe_w_down):
    D = D_MODEL
    n_p = BATCH * SEQ
    n_s = DEC_BATCH * DEC_SEQ
    seg = DEC_SEQ
    assert n_p == seg, "context tokens must fill exactly one conditioning segment"
    n_seg = 1 + DEC_BATCH

    x = jnp.concatenate([x_prompt.reshape(n_p, D), x_sample.reshape(n_s, D)], axis=0)
    cond8 = jnp.concatenate([c_ctx[None, :], c, jnp.zeros((8 - n_seg, D), F32)], axis=0)
    mod = _modulation(cond8, w_mod, b_mod).reshape(DEPTH, 8, 6, 1, D)
    gains = norm_gains.reshape(DEPTH, 4, 1, D)
    rope_cos, rope_sin = _rope_tables(DEC_SEQ)
    kpe_pad = jnp.concatenate(
        [cache_mla_kpe, jnp.zeros(cache_mla_kpe.shape[:-1] + (LANES - MLA_DR,), F32)], axis=-1)

    ckv_layers, kpe_layers, ret_layers = [], [], []
    for l in range(DEPTH):
        w_in_p = jnp.concatenate([w_in[l], jnp.zeros((D, ZW - IN_COLS), F32)], axis=1).astype(BF16)
        z = _inproj(x, gains[l], mod[l], w_in_p, seg)

        y_conv = _conv_module(z, conv_w[l], conv_b[l], conv_ln_g[l], conv_ln_b[l], n_p, SEQ,
                              DEC_SEQ)

        s0_all = jnp.concatenate(
            [jnp.zeros((BATCH, 2, RET_H, RET_DK, RET_DV), F32), state_ret[:, l]], axis=0)
        y_ret, s_fin = _retention(z, ret_decay_logit[l], ret_gn_g[l], s0_all, n_p, SEQ, DEC_SEQ)

        wq, wk, wv = _mla_weights(mla_w_uq[l], mla_w_ukv[l])
        qp, kp, v, ckv_n = _mla_prep(z, rope_cos, rope_sin, mla_q_norm[l], mla_kv_norm[l],
                                     wq, wk, wv, seg)
        k_ctx, v_ctx = _ctx_prep(cache_mla_ckv[:, l].reshape(DEC_BATCH * PAST_LEN, MLA_KV_LORA),
                                 kpe_pad[:, l].reshape(DEC_BATCH * PAST_LEN, LANES), wk, wv)
        y_mla_a = _attention(qp, kp, v, 0, BATCH, SEQ, nh=MLA_H, tq=SEQ)
        y_mla_b = _attention(qp, kp, v, n_p, DEC_BATCH, DEC_SEQ, ctx=(k_ctx, v_ctx), tq=1024,
                             tk=1024)

        ckv_layers.append(ckv_n[:n_p].reshape(BATCH, SEQ, MLA_KV_LORA))
        kpe_layers.append(z[:n_p, ZC_KR:ZC_KR + MLA_DR].reshape(BATCH, SEQ, MLA_DR))
        ret_layers.append(s_fin[:BATCH])

        w_out_b = w_out[l].astype(BF16)
        i = l // 2
        if l % 2 == 0:
            x1, h = _outproj(y_conv, y_ret, y_mla_a, y_mla_b, w_out_b, x, gains[l], mod[l], seg)
            x = _dense_ffn(h, ffn_w_gate[i].astype(BF16), ffn_w_up[i].astype(BF16),
                           ffn_w_down[i].astype(BF16), x1, gains[l], mod[l], seg)
        else:
            wr = moe_w_router[i].T
            wr_hi = wr.astype(BF16)
            wr_lo = (wr - wr_hi.astype(F32)).astype(BF16)
            router = (jnp.concatenate([wr_hi, wr_lo], axis=0),
                      jnp.concatenate([wr_hi, jnp.zeros_like(wr_hi)], axis=0),
                      moe_b_router[i][:, None])
            x1, h, route_t = _outproj(y_conv, y_ret, y_mla_a, y_mla_b, w_out_b, x, gains[l], mod[l],
                                      seg, router=router)
            route = route_t.T
            parts = _moe_layer(h, route, x1, moe_w_gate[i].astype(BF16), moe_w_up[i].astype(BF16),
                               moe_w_down[i].astype(BF16), gains[l], mod[l], seg, n_p)
            x = jnp.concatenate(parts, axis=0) if l + 1 < DEPTH else None
        if l + 1 == DEPTH and l % 2 == 0:
            parts = (x[:n_p], x[n_p:])

    y_prompt = parts[0].reshape(BATCH, SEQ, D)
    y_sample = parts[1].reshape(DEC_BATCH, DEC_SEQ, D)
    return (y_prompt, y_sample, jnp.stack(ckv_layers, axis=1), jnp.stack(kpe_layers, axis=1),
            jnp.stack(ret_layers, axis=1))
```

```python
import functools

import jax
import jax.numpy as jnp
from jax import lax
from jax.experimental import pallas as pl
from jax.experimental.pallas import tpu as pltpu

F32 = jnp.float32
BF16 = jnp.bfloat16

D_MODEL = 2048
BATCH = 16
SEQ = 256
DEPTH = 2
DEC_BATCH = 4
DEC_SEQ = 4096
PAST_LEN = 256
GRID_W = 64
EPS = 1e-6
CONV_C = 512
CONV_K = 31
RET_H = 4
RET_DK = 128
RET_DV = 128
MLA_H = 8
MLA_DN = 128
MLA_DR = 64
MLA_DV = 128
MLA_Q_LORA = 768
MLA_KV_LORA = 256
MLA_SCALE = (MLA_DN + MLA_DR) ** -0.5
LOG2E = 1.4426950408889634
ROPE_BASE = 10000.0
ROPE_AXIS = MLA_DR // 2
D_FF = 5632
N_EXPERTS = 8
TOP_K = 2
D_EXPERT = 7168

LANES = 128
SUBLANES = 8
HALO = 16
QK_PAD = 256
V_PAD = 256

ZC_CONV = 0
ZC_RQ = 2 * CONV_C
ZC_RK = ZC_RQ + RET_H * RET_DK
ZC_RV = ZC_RK + RET_H * RET_DK
ZC_RG = ZC_RV + RET_H * RET_DV
ZC_CQ = ZC_RG + RET_H * RET_DV
ZC_CKV = ZC_CQ + MLA_Q_LORA
ZC_KR = ZC_CKV + MLA_KV_LORA
IN_COLS = ZC_KR + MLA_DR
ZW = 4608


def _params(*sem):
    return pltpu.CompilerParams(dimension_semantics=sem)


def _silu(x):
    return x * jax.nn.sigmoid(x)


def _rms(x, g):
    return x * lax.rsqrt(jnp.mean(x * x, axis=-1, keepdims=True) + EPS) * g


def _mod_kernel(c_ref, w_ref, b_ref, o_ref):
    c = c_ref[...]
    a = _silu(c).astype(BF16)
    o_ref[...] = jnp.dot(a, w_ref[...].astype(BF16), preferred_element_type=F32) + b_ref[...]


def _modulation(cond8, w_mod, b_mod, tn=1024):
    L, D, N = w_mod.shape
    return pl.pallas_call(
        _mod_kernel,
        out_shape=jax.ShapeDtypeStruct((L, 8, N), F32),
        grid=(L, N // tn),
        in_specs=[
            pl.BlockSpec((8, D), lambda l, j: (0, 0)),
            pl.BlockSpec((None, D, tn), lambda l, j: (l, 0, j)),
            pl.BlockSpec((None, 1, tn), lambda l, j: (l, 0, j)),
        ],
        out_specs=pl.BlockSpec((None, 8, tn), lambda l, j: (l, 0, j)),
        compiler_params=_params("parallel", "arbitrary"),
        name="modulation",
    )(cond8, w_mod, b_mod.reshape(L, 1, N))


def _inproj_kernel(x_ref, g_ref, sc_ref, sh_ref, w_ref, o_ref, h_scr, r_scr, *, rb):
    @pl.when(pl.program_id(1) == 0)
    def _():
        x = x_ref[...]
        r = lax.rsqrt(jnp.mean(x * x, axis=-1, keepdims=True) + EPS)
        r_scr[...] = jnp.broadcast_to(r, r_scr.shape)
        a = g_ref[...] * (1.0 + sc_ref[...])
        b = sh_ref[...]
        for r0 in range(0, x_ref.shape[0], rb):
            rr = r_scr[r0:r0 + rb, :]
            for c0 in range(0, x_ref.shape[1], LANES):
                cols = slice(c0, c0 + LANES)
                y = x_ref[r0:r0 + rb, cols] * rr * a[:, cols] + b[:, cols]
                h_scr[r0:r0 + rb, cols] = y.astype(BF16)

    o_ref[...] = jnp.dot(h_scr[...], w_ref[...], preferred_element_type=F32)


def _mod_spec(chunk, tm, seg):
    return pl.BlockSpec((None, None, 1, D_MODEL), lambda i, *_: (i // (seg // tm), chunk, 0, 0))


def _gain_spec(k):
    return pl.BlockSpec((None, 1, D_MODEL), lambda i, *_: (k, 0, 0))


def _inproj(x, gains, mod, w_in_p, seg, tm=1024, tn=1536, rb=64):
    tok, D = x.shape
    zw = w_in_p.shape[1]
    return pl.pallas_call(
        functools.partial(_inproj_kernel, rb=rb),
        out_shape=jax.ShapeDtypeStruct((tok, zw), F32),
        grid=(tok // tm, zw // tn),
        in_specs=[
            pl.BlockSpec((tm, D), lambda i, j: (i, 0)),
            _gain_spec(0),
            _mod_spec(1, tm, seg),
            _mod_spec(0, tm, seg),
            pl.BlockSpec((D, tn), lambda i, j: (0, j)),
        ],
        out_specs=pl.BlockSpec((tm, tn), lambda i, j: (i, j)),
        scratch_shapes=[pltpu.VMEM((tm, D), BF16), pltpu.VMEM((tm, LANES), F32)],
        compiler_params=_params("parallel", "arbitrary"),
        name="inproj",
    )(x, gains, mod, mod, w_in_p)


def _seq_pos(t, n_a, per_a, per_b):
    in_a = t < n_a
    tb = t - n_a
    seq = jnp.where(in_a, t // per_a, n_a // per_a + tb // per_b)
    pos = jnp.where(in_a, t % per_a, tb % per_b)
    n = jnp.where(in_a, per_a, per_b)
    return seq, pos, n


def _conv_kernel(zc_ref, zp_ref, zn_ref, w_ref, b_ref, g_ref, bb_ref, o_ref, ubuf, cbuf, shifted,
                 *, tt, n_a, per_a, per_b, rows):
    _, pos, n = _seq_pos(pl.program_id(0), n_a, per_a, per_b)
    C = CONV_C

    def glu(z):
        return z[:, :C] * jax.nn.sigmoid(z[:, C:])

    ubuf[0:HALO, :] = jnp.where(pos == 0, 0.0, glu(zp_ref[...]))
    ubuf[HALO:HALO + tt, :] = glu(zc_ref[...])
    ubuf[HALO + tt:2 * HALO + tt, :] = jnp.where(pos == n - 1, 0.0, glu(zn_ref[...]))

    span = shifted.shape[1]
    for s in range(1, SUBLANES):
        shifted[s - 1] = ubuf[s:s + span, :]

    base = HALO - CONV_K // 2
    for cb in range(C // LANES):
        cols = slice(cb * LANES, (cb + 1) * LANES)
        w = w_ref[:, cols]
        bias = b_ref[:, cols]
        for rb in range(tt // rows):
            r0 = rb * rows
            acc = jnp.zeros((rows, LANES), F32)
            for k in range(CONV_K):
                s = (base + k) % SUBLANES
                a = r0 + base + k - s
                tap = ubuf[a:a + rows, cols] if s == 0 else shifted[s - 1, a:a + rows, cols]
                acc = acc + tap * w[k:k + 1, :]
            cbuf[r0:r0 + rows, cols] = acc + bias

    u = cbuf[...]
    mu = jnp.mean(u, axis=-1, keepdims=True)
    d = u - mu
    var = jnp.mean(d * d, axis=-1, keepdims=True)
    y = d * lax.rsqrt(var + EPS) * g_ref[...] + bb_ref[...]
    o_ref[...] = _silu(y).astype(o_ref.dtype)


def _conv_module(z, conv_w, conv_b, ln_g, ln_b, n_a_tok, len_a, len_b, tt=256, rows=64):
    tok = z.shape[0]
    C = CONV_C
    hb = tt // HALO
    nhb = tok // HALO
    w_p = jnp.concatenate([conv_w, jnp.zeros((32 - CONV_K, C), F32)], axis=0)
    kern = functools.partial(_conv_kernel, tt=tt, n_a=n_a_tok // tt, per_a=len_a // tt,
                             per_b=len_b // tt, rows=rows)
    vec = pl.BlockSpec((1, C), lambda i: (0, 0))
    return pl.pallas_call(
        kern,
        out_shape=jax.ShapeDtypeStruct((tok, C), BF16),
        grid=(tok // tt,),
        in_specs=[
            pl.BlockSpec((tt, 2 * C), lambda i: (i, 0)),
            pl.BlockSpec((HALO, 2 * C), lambda i: (jnp.maximum(i * hb - 1, 0), 0)),
            pl.BlockSpec((HALO, 2 * C), lambda i: (jnp.minimum((i + 1) * hb, nhb - 1), 0)),
            pl.BlockSpec((32, C), lambda i: (0, 0)),
            vec, vec, vec,
        ],
        out_specs=pl.BlockSpec((tt, C), lambda i: (i, 0)),
        scratch_shapes=[pltpu.VMEM((tt + 2 * HALO, C), F32), pltpu.VMEM((tt, C), F32),
                        pltpu.VMEM((SUBLANES - 1, tt + 2 * HALO - SUBLANES, C), F32)],
        compiler_params=_params("parallel"),
        name="conv_module",
    )(z, z, z, w_p, conv_b.reshape(1, C), ln_g.reshape(1, C), ln_b.reshape(1, C))


def _log_sigmoid(x):
    return -(jnp.maximum(-x, 0.0) + jnp.log(1.0 + jnp.exp(-jnp.abs(x))))


def _ret_kernel(q_ref, k_ref, v_ref, dl_ref, s0_ref, *rest, C, n_a, per_a, per_b, n_chunks,
                backward):
    if backward:
        of_ref, rg_ref, gn_ref, y_ref, sfin_ref, S, dmat, qd, kd = rest
        c = n_chunks - 1 - pl.program_id(0)
    else:
        o_ref, sfin_ref, S, dmat, qd, kd = rest
        c = pl.program_id(0)
    _, pos, n = _seq_pos(c, n_a, per_a, per_b)
    first = (pos == n - 1) if backward else (pos == 0)
    last = (pos == 0) if backward else (pos == n - 1)
    H, dk, dv = RET_H, RET_DK, RET_DV

    @pl.when(pl.program_id(0) == 0)
    def _():
        row = lax.broadcasted_iota(jnp.int32, (C, LANES), 0).astype(F32)
        col = lax.broadcasted_iota(jnp.int32, (C, LANES), 1).astype(F32)
        for h in range(H):
            lg = _log_sigmoid(dl_ref[h])[0:1, :]
            if backward:
                qd[h] = jnp.exp((C - row) * lg)
                kd[h] = jnp.exp(row * lg)
            else:
                qd[h] = jnp.exp((row + 1.0) * lg)
                kd[h] = jnp.exp((C - 1.0 - row) * lg)
            for cb in range(C // LANES):
                diff = row - (col + cb * LANES)
                if backward:
                    diff = -diff
                dm = jnp.where(diff >= 0, jnp.exp(jnp.maximum(diff, 0.0) * lg), 0.0)
                dmat[h, :, cb * LANES:(cb + 1) * LANES] = dm

    @pl.when(first)
    def _():
        S[...] = s0_ref[...]

    nt = (((1,), (1,)), ((), ()))
    tn = (((0,), (0,)), ((), ()))
    for h in range(H):
        q = q_ref[:, h * dk:(h + 1) * dk].astype(BF16)
        kf = k_ref[:, h * dk:(h + 1) * dk] * (dk ** -0.5)
        v = v_ref[:, h * dv:(h + 1) * dv].astype(BF16)
        lg = _log_sigmoid(dl_ref[h])[0:1, :]
        att = lax.dot_general(q, kf.astype(BF16), nt, preferred_element_type=F32) * dmat[h]
        s_old = S[h]
        o = jnp.dot(att.astype(BF16), v, preferred_element_type=F32)
        o = o + qd[h] * jnp.dot(q, s_old.astype(BF16), preferred_element_type=F32)
        kv = lax.dot_general((kf * kd[h]).astype(BF16), v, tn, preferred_element_type=F32)
        S[h] = jnp.exp(C * lg) * s_old + kv
        if backward:
            t = of_ref[:, h * dv:(h + 1) * dv] + o
            mu = jnp.mean(t, axis=-1, keepdims=True)
            d = t - mu
            var = jnp.mean(d * d, axis=-1, keepdims=True)
            yn = d * lax.rsqrt(var + EPS) * gn_ref[:, h * dv:(h + 1) * dv]
            y_ref[:, h * dv:(h + 1) * dv] = (yn * _silu(rg_ref[:, h * dv:(h + 1) * dv])).astype(
                y_ref.dtype)
        else:
            o_ref[:, h * dv:(h + 1) * dv] = o

    @pl.when(last)
    def _():
        sfin_ref[...] = S[...]


def _retention(z, decay_logit, gn_g, s0_all, n_a_tok, len_a, len_b, C=256):
    tok = z.shape[0]
    H, dk, dv = RET_H, RET_DK, RET_DV
    W = H * dk
    n_chunks = tok // C
    n_a, per_a, per_b = n_a_tok // C, len_a // C, len_b // C
    n_seq = s0_all.shape[0]
    dl = jnp.broadcast_to(decay_logit[:, :, None, None], (2, H, 8, LANES)).astype(F32)
    geo = dict(C=C, n_a=n_a, per_a=per_a, per_b=per_b, n_chunks=n_chunks)

    def order(backward):
        return (lambda g: n_chunks - 1 - g) if backward else (lambda g: g)

    def zspec(col0, backward):
        cidx = col0 // W
        f = order(backward)
        return pl.BlockSpec((C, W), lambda g: (f(g), cidx))

    def seq_of(backward):
        f = order(backward)
        return lambda g: _seq_pos(f(g), n_a, per_a, per_b)[0]

    def common_in(d):
        b = d == 1
        s = seq_of(b)
        return [
            zspec(ZC_RQ, b), zspec(ZC_RK, b), zspec(ZC_RV, b),
            pl.BlockSpec((None, H, 8, LANES), lambda g: (d, 0, 0, 0)),
            pl.BlockSpec((None, None, H, dk, dv), lambda g: (s(g), d, 0, 0, 0)),
        ]

    def sfin_spec(d):
        s = seq_of(d == 1)
        return pl.BlockSpec((None, H, dk, dv), lambda g: (s(g), 0, 0, 0))

    scratch = [pltpu.VMEM((H, dk, dv), F32), pltpu.VMEM((H, C, C), F32),
               pltpu.VMEM((H, C, LANES), F32), pltpu.VMEM((H, C, LANES), F32)]
    sfin_shape = jax.ShapeDtypeStruct((n_seq, H, dk, dv), F32)

    o_f, s_f = pl.pallas_call(
        functools.partial(_ret_kernel, backward=False, **geo),
        out_shape=(jax.ShapeDtypeStruct((tok, W), F32), sfin_shape),
        grid=(n_chunks,),
        in_specs=common_in(0),
        out_specs=(pl.BlockSpec((C, W), lambda g: (g, 0)), sfin_spec(0)),
        scratch_shapes=scratch,
        compiler_params=_params("arbitrary"),
        name="retention_fwd",
    )(z, z, z, dl, s0_all)

    rev = order(True)
    y, s_b = pl.pallas_call(
        functools.partial(_ret_kernel, backward=True, **geo),
        out_shape=(jax.ShapeDtypeStruct((tok, W), BF16), sfin_shape),
        grid=(n_chunks,),
        in_specs=common_in(1) + [
            pl.BlockSpec((C, W), lambda g: (rev(g), 0)),
            zspec(ZC_RG, True),
            pl.BlockSpec((1, W), lambda g: (0, 0)),
        ],
        out_specs=(pl.BlockSpec((C, W), lambda g: (rev(g), 0)), sfin_spec(1)),
        scratch_shapes=scratch,
        compiler_params=_params("arbitrary"),
        name="retention_bwd",
    )(z, z, z, dl, s0_all, o_f, z, gn_g.reshape(1, W))
    return y, jnp.stack([s_f, s_b], axis=1)


def _rope(x, cos, sin):
    lane = lax.broadcasted_iota(jnp.int32, x.shape, 1)
    fwd = pltpu.roll(x, LANES - 16, axis=1)
    bwd = pltpu.roll(x, 16, axis=1)
    partner = jnp.where((lane % 32) < 16, fwd, bwd)
    return x * cos + partner * sin


def _mla_prep_kernel(cq_ref, ckv_ref, kr_ref, cos_ref, sin_ref, qg_ref, kg_ref, wq_ref, wk_ref,
                     wv_ref, q_out, k_out, v_out, ckvn_out):
    cos = cos_ref[...]
    sin = sin_ref[...]
    cqn = _rms(cq_ref[...], qg_ref[...]).astype(BF16)
    qa = jnp.dot(cqn, wq_ref[...], preferred_element_type=F32) * (MLA_SCALE * LOG2E)
    ckvn = _rms(ckv_ref[...], kg_ref[...])
    ckvn_out[...] = ckvn
    kb = ckvn.astype(BF16)
    for h in range(MLA_H):
        c0 = h * QK_PAD
        q_out[:, c0:c0 + LANES] = qa[:, c0:c0 + LANES].astype(BF16)
        q_out[:, c0 + LANES:c0 + QK_PAD] = _rope(qa[:, c0 + LANES:c0 + QK_PAD], cos, sin).astype(
            BF16)
    _store_kv(kb, _rope(kr_ref[...], cos, sin), wk_ref, wv_ref, k_out, v_out)


def _store_kv(kb, kr, wk_ref, wv_ref, k_out, v_out):
    kn = jnp.dot(kb, wk_ref[...], preferred_element_type=F32)
    vv = jnp.dot(kb, wv_ref[...], preferred_element_type=F32)
    kr = kr.astype(BF16)
    ones = jnp.ones((kb.shape[0], V_PAD - MLA_DV), BF16)
    for h in range(MLA_H):
        k_out[:, h * QK_PAD:h * QK_PAD + LANES] = kn[:, h * MLA_DN:(h + 1) * MLA_DN].astype(BF16)
        k_out[:, h * QK_PAD + LANES:(h + 1) * QK_PAD] = kr
        v_out[:, h * V_PAD:h * V_PAD + MLA_DV] = vv[:, h * MLA_DV:(h + 1) * MLA_DV].astype(BF16)
        v_out[:, h * V_PAD + MLA_DV:(h + 1) * V_PAD] = ones


def _ctx_prep_kernel(ckv_ref, kr_ref, wk_ref, wv_ref, k_out, v_out):
    _store_kv(ckv_ref[...].astype(BF16), kr_ref[...], wk_ref, wv_ref, k_out, v_out)


def _mla_weights(w_uq, w_ukv):
    H, dn, dr, dv = MLA_H, MLA_DN, MLA_DR, MLA_DV
    wq = w_uq.reshape(-1, H, dn + dr)
    wq = jnp.concatenate([wq, jnp.zeros(wq.shape[:2] + (QK_PAD - dn - dr,), wq.dtype)], axis=-1)
    wkv = w_ukv.reshape(-1, H, dn + dv)
    wk = wkv[..., :dn].reshape(-1, H * dn)
    wv = wkv[..., dn:].reshape(-1, H * dv)
    return wq.reshape(-1, H * QK_PAD).astype(BF16), wk.astype(BF16), wv.astype(BF16)


def _mla_prep(z, rope_cos, rope_sin, q_norm, kv_norm, wq, wk, wv, seg, tm=512):
    tok = z.shape[0]
    H = MLA_H
    per = seg // tm
    tab = pl.BlockSpec((None, tm, LANES), lambda i: (jnp.minimum(i // per, 1), i % per, 0))
    full = lambda a: pl.BlockSpec(a.shape, lambda i: (0,) * a.ndim)
    qg = q_norm.reshape(1, -1)
    kg = kv_norm.reshape(1, -1)
    return pl.pallas_call(
        _mla_prep_kernel,
        out_shape=(jax.ShapeDtypeStruct((tok, H * QK_PAD), BF16),
                   jax.ShapeDtypeStruct((tok, H * QK_PAD), BF16),
                   jax.ShapeDtypeStruct((tok, H * V_PAD), BF16),
                   jax.ShapeDtypeStruct((tok, MLA_KV_LORA), F32)),
        grid=(tok // tm,),
        in_specs=[
            pl.BlockSpec((tm, MLA_Q_LORA), lambda i: (i, ZC_CQ // MLA_Q_LORA)),
            pl.BlockSpec((tm, MLA_KV_LORA), lambda i: (i, ZC_CKV // MLA_KV_LORA)),
            pl.BlockSpec((tm, LANES), lambda i: (i, ZC_KR // LANES)),
            tab, tab, full(qg), full(kg), full(wq), full(wk), full(wv),
        ],
        out_specs=(pl.BlockSpec((tm, H * QK_PAD), lambda i: (i, 0)),
                   pl.BlockSpec((tm, H * QK_PAD), lambda i: (i, 0)),
                   pl.BlockSpec((tm, H * V_PAD), lambda i: (i, 0)),
                   pl.BlockSpec((tm, MLA_KV_LORA), lambda i: (i, 0))),
        compiler_params=_params("parallel"),
        name="mla_prep",
    )(z, z, z, rope_cos, rope_sin, qg, kg, wq, wk, wv)


def _ctx_prep(ckv_ctx, kpe_ctx, wk, wv, tm=256):
    rows = ckv_ctx.shape[0]
    H = MLA_H
    full = lambda a: pl.BlockSpec(a.shape, lambda i: (0,) * a.ndim)
    return pl.pallas_call(
        _ctx_prep_kernel,
        out_shape=(jax.ShapeDtypeStruct((rows, H * QK_PAD), BF16),
                   jax.ShapeDtypeStruct((rows, H * V_PAD), BF16)),
        grid=(rows // tm,),
        in_specs=[pl.BlockSpec((tm, MLA_KV_LORA), lambda i: (i, 0)),
                  pl.BlockSpec((tm, LANES), lambda i: (i, 0)), full(wk), full(wv)],
        out_specs=(pl.BlockSpec((tm, H * QK_PAD), lambda i: (i, 0)),
                   pl.BlockSpec((tm, H * V_PAD), lambda i: (i, 0))),
        compiler_params=_params("parallel"),
        name="ctx_prep",
    )(ckv_ctx, kpe_ctx, wk, wv)


def _attn_kernel(*refs, nh, has_ctx, tk, rb):
    if has_ctx:
        q_ref, k_ref, v_ref, kc_ref, vc_ref, o_ref, *scr = refs
    else:
        q_ref, k_ref, v_ref, o_ref, *scr = refs
    s_bufs, p_bufs, m_bufs, acc = scr[0:2], scr[2:4], scr[4:6], scr[6]
    nt = (((1,), (1,)), ((), ()))
    tq = q_ref.shape[0]
    T = k_ref.shape[0]
    tk = min(tk, T)
    dv = MLA_DV
    buf = 0
    for hh in range(nh):
        qc = slice(hh * QK_PAD, (hh + 1) * QK_PAD)
        vc = slice(hh * V_PAD, (hh + 1) * V_PAD)
        q = q_ref[:, qc]
        chunks = [(kc_ref, vc_ref, 0, kc_ref.shape[0])] if has_ctx else []
        chunks += [(k_ref, v_ref, c0, tk) for c0 in range(0, T, tk)]
        m = None
        for kr, vr, c0, n in chunks:
            sb, pb, mb = s_bufs[buf], p_bufs[buf], m_bufs[buf]
            buf = 1 - buf
            sb[:, 0:n] = lax.dot_general(q, kr[c0:c0 + n, qc], nt, preferred_element_type=F32)
            m_new = jnp.max(sb[:, 0:n], axis=-1, keepdims=True)
            if m is not None:
                m_new = jnp.maximum(m, m_new)
                alpha = jnp.exp2(m - m_new)
            mb[...] = jnp.broadcast_to(m_new, (tq, LANES))
            for r0 in range(0, tq, rb):
                mrow = mb[r0:r0 + rb, :]
                for cb in range(0, n, LANES):
                    p = jnp.exp2(sb[r0:r0 + rb, cb:cb + LANES] - mrow)
                    pb[r0:r0 + rb, cb:cb + LANES] = p.astype(BF16)
            pv = jnp.dot(pb[:, 0:n], vr[c0:c0 + n, vc], preferred_element_type=F32)
            acc[...] = pv if m is None else alpha * acc[...] + pv
            m = m_new
        a = acc[...]
        o_ref[:, hh * dv:(hh + 1) * dv] = (a[:, 0:dv] / a[:, dv:2 * dv]).astype(o_ref.dtype)


def _attention(qp, kp, v, tok0, B, T, ctx=None, nh=1, tq=256, tk=512, rb=64):
    H, dv = MLA_H, MLA_DV
    s0 = tok0 // T
    q0 = tok0 // tq
    nq = T // tq
    in_specs = [
        pl.BlockSpec((tq, nh * QK_PAD), lambda b, h, i: (q0 + b * nq + i, h)),
        pl.BlockSpec((T, nh * QK_PAD), lambda b, h, i: (s0 + b, h)),
        pl.BlockSpec((T, nh * V_PAD), lambda b, h, i: (s0 + b, h)),
    ]
    args = [qp, kp, v]
    width = min(tk, T)
    if ctx is not None:
        past = ctx[0].shape[0] // B
        width = max(width, past)
        in_specs += [pl.BlockSpec((past, nh * QK_PAD), lambda b, h, i: (b, h)),
                     pl.BlockSpec((past, nh * V_PAD), lambda b, h, i: (b, h))]
        args += list(ctx)
    s_buf = pltpu.VMEM((tq, width), F32)
    p_buf = pltpu.VMEM((tq, width), BF16)
    m_buf = pltpu.VMEM((tq, LANES), F32)
    return pl.pallas_call(
        functools.partial(_attn_kernel, nh=nh, has_ctx=ctx is not None, tk=tk, rb=rb),
        out_shape=jax.ShapeDtypeStruct((B * T, H * dv), BF16),
        grid=(B, H // nh, nq),
        in_specs=in_specs,
        out_specs=pl.BlockSpec((tq, nh * dv), lambda b, h, i: (b * nq + i, h)),
        scratch_shapes=[s_buf, s_buf, p_buf, p_buf, m_buf, m_buf, pltpu.VMEM((tq, V_PAD), F32)],
        compiler_params=_params("parallel", "parallel", "arbitrary"),
        name="attention_ctx" if ctx is not None else "attention",
    )(*args)


def _route(h, wa_ref, wb_ref, br_ref):
    n = N_EXPERTS
    nt = (((1,), (1,)), ((), ()))
    h_hi = h.astype(BF16)
    h_lo = (h - h_hi.astype(F32)).astype(BF16)
    a = lax.dot_general(wa_ref[...], h_hi, nt, preferred_element_type=F32)
    b = lax.dot_general(wb_ref[...], h_lo, nt, preferred_element_type=F32)
    logits = a[0:n, :] + a[n:2 * n, :] + b[0:n, :] + br_ref[...]
    sub = lax.broadcasted_iota(jnp.int32, logits.shape, 0).astype(F32)
    neg = jnp.float32(-jnp.inf)

    def top(vals):
        m = jnp.max(vals, axis=0, keepdims=True)
        return m, jnp.min(jnp.where(vals == m, sub, float(n)), axis=0, keepdims=True)

    m1, i1 = top(logits)
    m2, i2 = top(jnp.where(sub == i1, neg, logits))
    e = jnp.exp(m2 - m1)
    g1 = 1.0 / (1.0 + e)
    g2 = e / (1.0 + e)
    rec = jnp.where(sub == 0, i1, 0.0)
    rec = jnp.where(sub == 1, i2, rec)
    rec = jnp.where(sub == 2, g1, rec)
    return jnp.where(sub == 3, g2, rec)


def _outproj_kernel(yc_ref, yr_ref, yma_ref, ymb_ref, w_ref, x_ref, gate_ref, n1_ref, n2_ref,
                    sc_ref, sh_ref, *rest, routed, n_a):
    if routed:
        wa_ref, wb_ref, br_ref, x1_ref, h_ref, r_ref = rest
    else:
        x1_ref, h_ref = rest
    c1 = yc_ref.shape[1]
    c2 = c1 + yr_ref.shape[1]
    ym = jnp.where(pl.program_id(0) < n_a, yma_ref[...], ymb_ref[...])
    y = jnp.dot(yc_ref[...], w_ref[0:c1, :], preferred_element_type=F32)
    y = y + jnp.dot(yr_ref[...], w_ref[c1:c2, :], preferred_element_type=F32)
    y = y + jnp.dot(ym, w_ref[c2:, :], preferred_element_type=F32)
    x1 = x_ref[...] + gate_ref[...] * _rms(y, n1_ref[...])
    x1_ref[...] = x1
    h = _rms(x1, n2_ref[...]) * (1.0 + sc_ref[...]) + sh_ref[...]
    h_ref[...] = h.astype(h_ref.dtype)
    if routed:
        r_ref[...] = _route(h, wa_ref, wb_ref, br_ref)


def _outproj(yc, yr, yma, ymb, w_out, x, gains, mod, seg, router=None, tm=512):
    tok, D = x.shape
    routed = router is not None
    n_a = yma.shape[0] // tm
    row = lambda a: pl.BlockSpec((tm, a.shape[1]), lambda i: (i, 0))
    full = lambda a: pl.BlockSpec(a.shape, lambda i: (0,) * a.ndim)
    part_a = pl.BlockSpec((tm, yma.shape[1]), lambda i: (jnp.minimum(i, n_a - 1), 0))
    part_b = pl.BlockSpec((tm, ymb.shape[1]), lambda i: (jnp.maximum(i - n_a, 0), 0))
    in_specs = [row(yc), row(yr), part_a, part_b, full(w_out), row(x), _mod_spec(2, tm, seg),
                _gain_spec(1), _gain_spec(2), _mod_spec(4, tm, seg), _mod_spec(3, tm, seg)]
    args = [yc, yr, yma, ymb, w_out, x, mod, gains, gains, mod, mod]
    out_shape = [jax.ShapeDtypeStruct((tok, D), F32),
                 jax.ShapeDtypeStruct((tok, D), F32 if routed else BF16)]
    out_specs = [pl.BlockSpec((tm, D), lambda i: (i, 0)), pl.BlockSpec((tm, D), lambda i: (i, 0))]
    if routed:
        in_specs += [full(a) for a in router]
        args += list(router)
        out_shape.append(jax.ShapeDtypeStruct((SUBLANES, tok), F32))
        out_specs.append(pl.BlockSpec((SUBLANES, tm), lambda i: (0, i)))
    return pl.pallas_call(
        functools.partial(_outproj_kernel, routed=routed, n_a=n_a),
        out_shape=tuple(out_shape),
        grid=(tok // tm,),
        in_specs=in_specs,
        out_specs=tuple(out_specs),
        compiler_params=_params("parallel"),
        name="outproj_routed" if routed else "outproj",
    )(*args)


def _ffn_kernel(h_ref, wg_ref, wu_ref, wd_ref, x_ref, gate_ref, n_ref, o_ref, acc):
    f = pl.program_id(1)

    @pl.when(f == 0)
    def _():
        acc[...] = jnp.zeros_like(acc)

    h = h_ref[...]
    g = jnp.dot(h, wg_ref[...], preferred_element_type=F32)
    u = jnp.dot(h, wu_ref[...], preferred_element_type=F32)
    acc[...] += jnp.dot((_silu(g) * u).astype(BF16), wd_ref[...], preferred_element_type=F32)

    @pl.when(f == pl.num_programs(1) - 1)
    def _():
        o_ref[...] = x_ref[...] + gate_ref[...] * _rms(acc[...], n_ref[...])


def _dense_ffn(h, wg, wu, wd, x, gains, mod, seg, tm=512, tf=512):
    tok, D = x.shape
    F = wg.shape[1]
    return pl.pallas_call(
        _ffn_kernel,
        out_shape=jax.ShapeDtypeStruct((tok, D), F32),
        grid=(tok // tm, F // tf),
        in_specs=[
            pl.BlockSpec((tm, D), lambda i, f: (i, 0)),
            pl.BlockSpec((D, tf), lambda i, f: (0, f)),
            pl.BlockSpec((D, tf), lambda i, f: (0, f)),
            pl.BlockSpec((tf, D), lambda i, f: (f, 0)),
            pl.BlockSpec((tm, D), lambda i, f: (i, 0)),
            _mod_spec(5, tm, seg),
            _gain_spec(3),
        ],
        out_specs=pl.BlockSpec((tm, D), lambda i, f: (i, 0)),
        scratch_shapes=[pltpu.VMEM((tm, D), F32)],
        compiler_params=_params("parallel", "arbitrary"),
        name="dense_ffn",
    )(h, wg, wu, wd, x, mod, gains)


def _moe_ffn_kernel(te_ref, nu_ref, tok_ref, h_ref, wg_ref, wu_ref, wd_ref, o_ref, stage, xb, acc,
                    sem, *, tm, chunk):
    i = pl.program_id(0)
    f = pl.program_id(1)
    nf = pl.num_programs(1)
    nu = nu_ref[0]
    used = i < nu
    slot = i % 2

    def request(tile, sl, lo, count):
        def body(r, carry):
            pltpu.make_async_copy(h_ref.at[pl.ds(tok_ref[tile * tm + lo + r], 1), :],
                                  stage.at[sl, pl.ds(lo + r, 1), :], sem.at[sl]).start()
            return carry
        lax.fori_loop(0, count, body, 0, unroll=SUBLANES)

    @pl.when((i == 0) & (f == 0))
    def _():
        request(0, 0, 0, tm)

    @pl.when(used & (f == 0))
    def _():
        pltpu.make_async_copy(h_ref.at[pl.ds(0, tm), :], stage.at[slot], sem.at[slot]).wait()
        xb[...] = stage[slot].astype(BF16)
        acc[...] = jnp.zeros_like(acc)

    full_steps = tm // chunk

    @pl.when((i + 1 < nu) & (f < full_steps))
    def _():
        request(i + 1, 1 - slot, f * chunk, chunk)

    if tm % chunk:
        @pl.when((i + 1 < nu) & (f == full_steps))
        def _():
            request(i + 1, 1 - slot, full_steps * chunk, tm % chunk)

    @pl.when(used)
    def _():
        x = xb[...]
        g = jnp.dot(x, wg_ref[...], preferred_element_type=F32)
        u = jnp.dot(x, wu_ref[...], preferred_element_type=F32)
        acc[...] += jnp.dot((_silu(g) * u).astype(BF16), wd_ref[...], preferred_element_type=F32)

    @pl.when(used & (f == nf - 1))
    def _():
        o_ref[...] = acc[...]

    @pl.when(jnp.logical_not(used) & (f == nf - 1))
    def _():
        o_ref[...] = jnp.zeros_like(o_ref)


def _moe_ffn(h, slot_token, tile_expert, n_used, wg, wu, wd, tm, tf=1024):
    D = h.shape[1]
    n_slots = slot_token.shape[0]
    F = wg.shape[2]
    nf = F // tf
    n_tiles = n_slots // tm

    def fi(i, f, nu):
        return jnp.where(i < nu[0], f, nf - 1)

    chunk = pl.cdiv(pl.cdiv(tm, nf), SUBLANES) * SUBLANES
    assert pl.cdiv(tm, chunk) <= nf and tm % SUBLANES == 0
    return pl.pallas_call(
        functools.partial(_moe_ffn_kernel, tm=tm, chunk=chunk),
        out_shape=jax.ShapeDtypeStruct((n_slots, D), F32),
        grid_spec=pltpu.PrefetchScalarGridSpec(
            num_scalar_prefetch=3,
            grid=(n_tiles, nf),
            in_specs=[
                pl.BlockSpec(memory_space=pl.ANY),
                pl.BlockSpec((None, D, tf), lambda i, f, te, nu, tk: (te[i], 0, fi(i, f, nu))),
                pl.BlockSpec((None, D, tf), lambda i, f, te, nu, tk: (te[i], 0, fi(i, f, nu))),
                pl.BlockSpec((None, tf, D), lambda i, f, te, nu, tk: (te[i], fi(i, f, nu), 0)),
            ],
            out_specs=pl.BlockSpec((tm, D), lambda i, f, te, nu, tk: (i, 0)),
            scratch_shapes=[pltpu.VMEM((2, tm, D), F32), pltpu.VMEM((tm, D), BF16),
                            pltpu.VMEM((tm, D), F32), pltpu.SemaphoreType.DMA((2,))],
        ),
        compiler_params=_params("arbitrary", "arbitrary"),
        name="moe_ffn",
    )(tile_expert, n_used, slot_token, h, wg, wu, wd)


def _combine_kernel(d_ref, ys_ref, gates_ref, x_ref, gate_ref, n_ref, oa_ref, ob_ref, b0, b1, sem,
                    *, rows, n_a):
    i = pl.program_id(0)
    n = pl.num_programs(0)
    slot = i % 2

    def request(step, sl):
        def body(r, carry):
            a = 2 * (step * rows + r)
            pltpu.make_async_copy(ys_ref.at[pl.ds(d_ref[a], 1), :],
                                  b0.at[sl, pl.ds(r, 1), :], sem.at[0, sl]).start()
            pltpu.make_async_copy(ys_ref.at[pl.ds(d_ref[a + 1], 1), :],
                                  b1.at[sl, pl.ds(r, 1), :], sem.at[1, sl]).start()
            return carry
        lax.fori_loop(0, rows, body, 0, unroll=SUBLANES)

    @pl.when(i == 0)
    def _():
        request(0, 0)

    @pl.when(i + 1 < n)
    def _():
        request(i + 1, 1 - slot)

    pltpu.make_async_copy(ys_ref.at[pl.ds(0, rows), :], b0.at[slot], sem.at[0, slot]).wait()
    pltpu.make_async_copy(ys_ref.at[pl.ds(0, rows), :], b1.at[slot], sem.at[1, slot]).wait()
    gates = gates_ref[...]
    out = gates[:, 2:3] * b0[slot] + gates[:, 3:4] * b1[slot]
    res = x_ref[...] + gate_ref[...] * _rms(out, n_ref[...])

    @pl.when(i < n_a)
    def _():
        oa_ref[...] = res

    @pl.when(i >= n_a)
    def _():
        ob_ref[...] = res


def _moe_combine(ys, dest, route, x, gains, mod, seg, n_a_tok, rows=256):
    tok, D = x.shape
    n_a = n_a_tok // rows
    return pl.pallas_call(
        functools.partial(_combine_kernel, rows=rows, n_a=n_a),
        out_shape=(jax.ShapeDtypeStruct((n_a_tok, D), F32),
                   jax.ShapeDtypeStruct((tok - n_a_tok, D), F32)),
        grid_spec=pltpu.PrefetchScalarGridSpec(
            num_scalar_prefetch=1,
            grid=(tok // rows,),
            in_specs=[
                pl.BlockSpec(memory_space=pl.ANY),
                pl.BlockSpec((rows, SUBLANES), lambda i, d: (i, 0)),
                pl.BlockSpec((rows, D), lambda i, d: (i, 0)),
                _mod_spec(5, rows, seg),
                _gain_spec(3),
            ],
            out_specs=(pl.BlockSpec((rows, D), lambda i, d: (jnp.minimum(i, n_a - 1), 0)),
                       pl.BlockSpec((rows, D), lambda i, d: (jnp.maximum(i - n_a, 0), 0))),
            scratch_shapes=[pltpu.VMEM((2, rows, D), F32), pltpu.VMEM((2, rows, D), F32),
                            pltpu.SemaphoreType.DMA((2, 2))],
        ),
        compiler_params=_params("arbitrary"),
        name="moe_combine",
    )(dest, ys, route, x, mod, gains)


def _moe_plan(route, tm):
    tok = route.shape[0]
    n_assign = tok * TOP_K
    n_tiles = -(-n_assign // tm) + N_EXPERTS
    n_slots = n_tiles * tm
    expert = route[:, :TOP_K].astype(jnp.int32).reshape(-1)
    token = jnp.repeat(jnp.arange(tok, dtype=jnp.int32), TOP_K)
    onehot = (expert[:, None] == jnp.arange(N_EXPERTS, dtype=jnp.int32)[None, :]).astype(jnp.int32)
    csum = jnp.cumsum(onehot, axis=0)
    counts = csum[-1]
    padded = (counts + tm - 1) // tm * tm
    padded_end = jnp.cumsum(padded)
    padded_start = padded_end - padded
    dest = jnp.sum(onehot * (csum - 1 + padded_start[None, :]), axis=1).astype(jnp.int32)
    slot_token = jnp.zeros((n_slots,), jnp.int32).at[dest].set(token)
    n_used = (padded_end[-1] // tm).astype(jnp.int32)
    tile_start = jnp.minimum(jnp.arange(n_tiles, dtype=jnp.int32), n_used - 1) * tm
    tile_expert = jnp.minimum(
        jnp.sum((tile_start[:, None] >= padded_end[None, :]).astype(jnp.int32), axis=1),
        N_EXPERTS - 1)
    return slot_token, dest, tile_expert, n_used.reshape(1)


def _moe_layer(h, route, x, wg, wu, wd, gains, mod, seg, n_a_tok, tm=512):
    slot_token, dest, tile_expert, n_used = _moe_plan(route, tm)
    ys = _moe_ffn(h, slot_token, tile_expert, n_used, wg, wu, wd, tm)
    return _moe_combine(ys, dest, route, x, gains, mod, seg, n_a_tok)


def _rope_tables(n_tokens):
    half = ROPE_AXIS // 2
    t = jnp.arange(n_tokens)
    row = (t // GRID_W).astype(F32)
    col = (t % GRID_W).astype(F32)
    inv_freq = jnp.power(ROPE_BASE, -jnp.arange(0, ROPE_AXIS, 2, dtype=F32) / ROPE_AXIS)
    ar = row[:, None] * inv_freq
    ac = col[:, None] * inv_freq
    pad = LANES - MLA_DR
    cos = jnp.concatenate([jnp.cos(ar), jnp.cos(ar), jnp.cos(ac), jnp.cos(ac),
                           jnp.ones((n_tokens, pad), F32)], axis=1)
    sin = jnp.concatenate([-jnp.sin(ar), jnp.sin(ar), -jnp.sin(ac), jnp.sin(ac),
                           jnp.zeros((n_tokens, pad), F32)], axis=1)
    assert cos.shape[1] == LANES and half * 4 == MLA_DR
    ident = (jnp.ones_like(cos), jnp.zeros_like(sin))
    return jnp.stack([ident[0], cos]), jnp.stack([ident[1], sin])


def kernel(x_prompt, x_sample, cache_mla_ckv, cache_mla_kpe, state_ret, c, c_ctx, w_mod, b_mod, norm_gains, w_in, w_out, conv_w, conv_b, conv_ln_g, conv_ln_b, ret_decay_logit, ret_gn_g, mla_q_norm, mla_w_uq, mla_kv_norm, mla_w_ukv, ffn_w_gate, ffn_w_up, ffn_w_down, moe_w_router, moe_b_router, moe_w_gate, moe_w_up, moe_w_down):
    D = D_MODEL
    n_p = BATCH * SEQ
    n_s = DEC_BATCH * DEC_SEQ
    seg = DEC_SEQ
    assert n_p == seg, "context tokens must fill exactly one conditioning segment"
    n_seg = 1 + DEC_BATCH

    x = jnp.concatenate([x_prompt.reshape(n_p, D), x_sample.reshape(n_s, D)], axis=0)
    cond8 = jnp.concatenate([c_ctx[None, :], c, jnp.zeros((8 - n_seg, D), F32)], axis=0)
    mod = _modulation(cond8, w_mod, b_mod).reshape(DEPTH, 8, 6, 1, D)
    gains = norm_gains.reshape(DEPTH, 4, 1, D)
    rope_cos, rope_sin = _rope_tables(DEC_SEQ)
    kpe_pad = jnp.concatenate(
        [cache_mla_kpe, jnp.zeros(cache_mla_kpe.shape[:-1] + (LANES - MLA_DR,), F32)], axis=-1)

    ckv_layers, kpe_layers, ret_layers = [], [], []
    for l in range(DEPTH):
        w_in_p = jnp.concatenate([w_in[l], jnp.zeros((D, ZW - IN_COLS), F32)], axis=1).astype(BF16)
        z = _inproj(x, gains[l], mod[l], w_in_p, seg)

        y_conv = _conv_module(z, conv_w[l], conv_b[l], conv_ln_g[l], conv_ln_b[l], n_p, SEQ,
                              DEC_SEQ)

        s0_all = jnp.concatenate(
            [jnp.zeros((BATCH, 2, RET_H, RET_DK, RET_DV), F32), state_ret[:, l]], axis=0)
        y_ret, s_fin = _retention(z, ret_decay_logit[l], ret_gn_g[l], s0_all, n_p, SEQ, DEC_SEQ)

        wq, wk, wv = _mla_weights(mla_w_uq[l], mla_w_ukv[l])
        qp, kp, v, ckv_n = _mla_prep(z, rope_cos, rope_sin, mla_q_norm[l], mla_kv_norm[l],
                                     wq, wk, wv, seg)
        k_ctx, v_ctx = _ctx_prep(cache_mla_ckv[:, l].reshape(DEC_BATCH * PAST_LEN, MLA_KV_LORA),
                                 kpe_pad[:, l].reshape(DEC_BATCH * PAST_LEN, LANES), wk, wv)
        y_mla_a = _attention(qp, kp, v, 0, BATCH, SEQ, nh=MLA_H, tq=SEQ)
        y_mla_b = _attention(qp, kp, v, n_p, DEC_BATCH, DEC_SEQ, ctx=(k_ctx, v_ctx), tq=1024,
                             tk=1024)

        ckv_layers.append(ckv_n[:n_p].reshape(BATCH, SEQ, MLA_KV_LORA))
        kpe_layers.append(z[:n_p, ZC_KR:ZC_KR + MLA_DR].reshape(BATCH, SEQ, MLA_DR))
        ret_layers.append(s_fin[:BATCH])

        w_out_b = w_out[l].astype(BF16)
        i = l // 2
        if l % 2 == 0:
            x1, h = _outproj(y_conv, y_ret, y_mla_a, y_mla_b, w_out_b, x, gains[l], mod[l], seg)
            x = _dense_ffn(h, ffn_w_gate[i].astype(BF16), ffn_w_up[i].astype(BF16),
                           ffn_w_down[i].astype(BF16), x1, gains[l], mod[l], seg)
        else:
            wr = moe_w_router[i].T
            wr_hi = wr.astype(BF16)
            wr_lo = (wr - wr_hi.astype(F32)).astype(BF16)
            router = (jnp.concatenate([wr_hi, wr_lo], axis=0),
                      jnp.concatenate([wr_hi, jnp.zeros_like(wr_hi)], axis=0),
                      moe_b_router[i][:, None])
            x1, h, route_t = _outproj(y_conv, y_ret, y_mla_a, y_mla_b, w_out_b, x, gains[l], mod[l],
                                      seg, router=router)
            route = route_t.T
            parts = _moe_layer(h, route, x1, moe_w_gate[i].astype(BF16), moe_w_up[i].astype(BF16),
                               moe_w_down[i].astype(BF16), gains[l], mod[l], seg, n_p)
            x = jnp.concatenate(parts, axis=0) if l + 1 < DEPTH else None
        if l + 1 == DEPTH and l % 2 == 0:
            parts = (x[:n_p], x[n_p:])

    y_prompt = parts[0].reshape(BATCH, SEQ, D)
    y_sample = parts[1].reshape(DEC_BATCH, DEC_SEQ, D)
    return (y_prompt, y_sample, jnp.stack(ckv_layers, axis=1), jnp.stack(kpe_layers, axis=1),
            jnp.stack(ret_layers, axis=1))
```
